```python
import math
import jax, jax.numpy as jnp
from jax import lax
import numpy as np

D_MODEL = 2048
BATCH = 2
SEQ = 4096
DEPTH = 4

HEAD_DIM = 128
N_DIFF_HEADS = 4
N_NA_HEADS = 4
N_MEM_HEADS = 4
N_MEM = 256
DIFF_V_DIM = 2 * HEAD_DIM
DIFF_WIDTH = N_DIFF_HEADS * DIFF_V_DIM
NA_WIDTH = N_NA_HEADS * HEAD_DIM
MEM_WIDTH = N_MEM_HEADS * HEAD_DIM
MIX_WIDTH = DIFF_WIDTH + NA_WIDTH + MEM_WIDTH
DQ_W = N_DIFF_HEADS * 2 * HEAD_DIM
DK_W = N_DIFF_HEADS * 2 * HEAD_DIM
DV_W = DIFF_WIDTH
OFF_DQ = 0
OFF_DK = OFF_DQ + DQ_W
OFF_DV = OFF_DK + DK_W
OFF_NQ = OFF_DV + DV_W
OFF_NK = OFF_NQ + NA_WIDTH
OFF_NV = OFF_NK + NA_WIDTH
OFF_MQ = OFF_NV + NA_WIDTH
IN_WIDTH = OFF_MQ + MEM_WIDTH
ROPE_THETA = 500000.0
ROT_DIM = HEAD_DIM // 4
Q_BLOCK = 128
GRID_W = 64
MAX_WIN_H = 8
WIN_W = 16
PEER_HEADS = 8
N_KEYS = 128
N_EXPERTS = N_KEYS * N_KEYS
PEER_HALF = 128
PEER_KEY_DIM = 2 * PEER_HALF
PEER_TOPK = 16
PEER_TOK_BLOCK = 128
EPS = 1e-6

kernel_name = "hybrid_diffattn_natten_mem_peer_encoder"


def rms_norm(x, g):
    xf = x.astype(jnp.float32)
    y = xf * lax.rsqrt(jnp.mean(xf * xf, axis=-1, keepdims=True) + EPS) * g.astype(jnp.float32)
    return y.astype(x.dtype)


def rope_tables(seq):
    inv = ROPE_THETA ** (-jnp.arange(0, ROT_DIM, 2, dtype=jnp.float32) / ROT_DIM)
    ang = jnp.arange(seq, dtype=jnp.float32)[:, None] * inv[None, :]
    return jnp.cos(ang), jnp.sin(ang)


def partial_rope(x, cos, sin):
    half = ROT_DIM // 2
    shp = (cos.shape[0],) + (1,) * (x.ndim - 3) + (half,)
    c = cos.reshape(shp)
    s = sin.reshape(shp)
    xf = x.astype(jnp.float32)
    x1 = xf[..., :half]
    x2 = xf[..., half:ROT_DIM]
    out = jnp.concatenate([x1 * c - x2 * s, x2 * c + x1 * s, xf[..., ROT_DIM:]], axis=-1)
    return out.astype(x.dtype)


def diff_attention(q, k, v, lam, sub_g, lam_init):
    B, S = q.shape[0], q.shape[1]
    nb = S // Q_BLOCK
    scale = HEAD_DIM ** -0.5
    qb = q.reshape(B, nb, Q_BLOCK, N_DIFF_HEADS, 2, HEAD_DIM).swapaxes(0, 1)

    def block(qblk):
        s = jnp.einsum('bqhcd,bkhcd->bhcqk', qblk, k).astype(jnp.float32) * scale
        p = jax.nn.softmax(s, axis=-1)
        pd = p[:, :, 0] - lam * p[:, :, 1]
        return jnp.einsum('bhqk,bkhe->bqhe', pd.astype(v.dtype), v)

    o = lax.map(block, qb)
    o = o.swapaxes(0, 1).reshape(B, S, N_DIFF_HEADS, DIFF_V_DIM)
    o = rms_norm(o, sub_g) * (1.0 - lam_init)
    return o.reshape(B, S, DIFF_WIDTH)


def neighbourhood_attention(q, k, v, bias):
    B, S = q.shape[0], q.shape[1]
    R = S // GRID_W
    wh = min(MAX_WIN_H, R)
    scale = HEAD_DIM ** -0.5
    qg = q.reshape(B, R, GRID_W, N_NA_HEADS, HEAD_DIM)
    kg = k.reshape(B, R, GRID_W, N_NA_HEADS, HEAD_DIM)
    vg = v.reshape(B, R, GRID_W, N_NA_HEADS, HEAD_DIM)
    cols = jnp.arange(GRID_W)
    cs = jnp.clip(cols - WIN_W // 2, 0, GRID_W - WIN_W)
    cidx = cs[:, None] + jnp.arange(WIN_W)[None, :]
    dcol = cidx - cols[:, None] + (WIN_W - 1)

    def row(args):
        q_r, r = args
        rs = jnp.clip(r - wh // 2, 0, R - wh)
        kb = lax.dynamic_slice_in_dim(kg, rs, wh, axis=1)[:, :, cidx]
        vb = lax.dynamic_slice_in_dim(vg, rs, wh, axis=1)[:, :, cidx]
        drow = rs + jnp.arange(wh) - r + (MAX_WIN_H - 1)
        b_r = bias[:, drow[:, None, None], dcol[None, :, :]]
        s = jnp.einsum('bqhd,biqjhd->bhiqj', q_r, kb).astype(jnp.float32) * scale
        s = s + b_r.astype(jnp.float32)
        p = jax.nn.softmax(s, axis=(2, 4))
        return jnp.einsum('bhiqj,biqjhd->bqhd', p.astype(vb.dtype), vb)

    o = lax.map(row, (qg.swapaxes(0, 1), jnp.arange(R)))
    return o.swapaxes(0, 1).reshape(B, S, NA_WIDTH)


def memory_attention(q, mem_h, w_kv, gk):
    B, M = mem_h.shape[0], mem_h.shape[1]
    kv = (mem_h @ w_kv).reshape(B, M, 2, N_MEM_HEADS, HEAD_DIM)
    mk = rms_norm(kv[:, :, 0], gk)
    mv = kv[:, :, 1]
    s = jnp.einsum('bshd,bmhd->bhsm', q, mk).astype(jnp.float32) * (HEAD_DIM ** -0.5)
    p = jax.nn.softmax(s, axis=-1)
    o = jnp.einsum('bhsm,bmhd->bshd', p.astype(mv.dtype), mv)
    return o.reshape(q.shape[0], q.shape[1], MEM_WIDTH)


def peer_layer(h, w_pq, keys, u_tab, v_tab):
    B, S, D = h.shape
    q = (h @ w_pq).reshape(B, S, PEER_HEADS, 2, PEER_HALF)
    s = jnp.einsum('bshcd,hcnd->bshcn', q, keys).astype(jnp.float32)
    st, it = lax.top_k(s, PEER_TOPK)
    cand_s = (st[..., 0, :, None] + st[..., 1, None, :]).reshape(B, S, PEER_HEADS, PEER_TOPK * PEER_TOPK)
    cand_i = (it[..., 0, :, None] * N_KEYS + it[..., 1, None, :]).reshape(B, S, PEER_HEADS, PEER_TOPK * PEER_TOPK)
    top_s, pos = lax.top_k(cand_s, PEER_TOPK)
    ids = jnp.take_along_axis(cand_i, pos, axis=-1)
    g = jax.nn.softmax(top_s, axis=-1)
    T = B * S
    nblk = T // PEER_TOK_BLOCK
    hb = h.reshape(nblk, PEER_TOK_BLOCK, D)
    idb = ids.reshape(nblk, PEER_TOK_BLOCK, PEER_HEADS * PEER_TOPK)
    gb = g.reshape(nblk, PEER_TOK_BLOCK, PEER_HEADS * PEER_TOPK)

    def block(args):
        hx, idx, gx = args
        a = jnp.einsum('td,tkd->tk', hx, u_tab[idx]).astype(jnp.float32)
        a = jax.nn.gelu(a, approximate=False)
        return jnp.einsum('tk,tkd->td', (gx * a).astype(hx.dtype), v_tab[idx])

    out = lax.map(block, (hb, idb, gb))
    return out.reshape(B, S, D)


def setup_inputs(seed: int = 0) -> dict:
    key = jax.random.key(seed)
    ks = jax.random.split(key, 17)
    f32 = jnp.float32
    D = D_MODEL
    nrm = lambda k, shp, sc: jax.random.normal(k, shp, f32) * sc
    gain = lambda k, shp: 1.0 + 0.02 * jax.random.normal(k, shp, f32)
    return {
        "x": nrm(ks[0], (BATCH, SEQ, D), 1.0),
        "mem": nrm(ks[1], (BATCH, N_MEM, D), 1.0),
        "attn_norm": gain(ks[2], (DEPTH, D)),
        "w_in": nrm(ks[3], (DEPTH, D, IN_WIDTH), D ** -0.5),
        "qk_gain": gain(ks[4], (DEPTH, 6, HEAD_DIM)),
        "lambda_qk": nrm(ks[5], (DEPTH, 4, HEAD_DIM), 0.1),
        "subln_gain": gain(ks[6], (DEPTH, DIFF_V_DIM)),
        "na_bias": nrm(ks[7], (DEPTH, N_NA_HEADS, 2 * MAX_WIN_H - 1, 2 * WIN_W - 1), 0.1),
        "mem_norm": gain(ks[8], (DEPTH, D)),
        "w_mem_kv": nrm(ks[9], (DEPTH, D, 2 * MEM_WIDTH), D ** -0.5),
        "w_out": nrm(ks[10], (DEPTH, MIX_WIDTH, D), MIX_WIDTH ** -0.5),
        "ffn_norm": gain(ks[11], (DEPTH, D)),
        "w_pq": nrm(ks[12], (DEPTH, D, PEER_HEADS * PEER_KEY_DIM), D ** -0.5),
        "peer_keys": nrm(ks[13], (DEPTH, PEER_HEADS, 2, N_KEYS, PEER_HALF), PEER_HALF ** -0.5),
        "peer_u": nrm(ks[14], (DEPTH, N_EXPERTS, D), D ** -0.5),
        "peer_v": nrm(ks[15], (DEPTH, N_EXPERTS, D), 0.25),
    }


def reference(x, mem, attn_norm, w_in, qk_gain, lambda_qk, subln_gain, na_bias,
              mem_norm, w_mem_kv, w_out, ffn_norm, w_pq, peer_keys, peer_u, peer_v):
    B, S, _ = x.shape
    cos, sin = rope_tables(S)
    for l in range(DEPTH):
        lam_init = 0.8 - 0.6 * math.exp(-0.3 * l)
        h = rms_norm(x, attn_norm[l])
        proj = h @ w_in[l]
        gq = qk_gain[l]
        dq = proj[..., OFF_DQ:OFF_DK].reshape(B, S, N_DIFF_HEADS, 2, HEAD_DIM)
        dk = proj[..., OFF_DK:OFF_DV].reshape(B, S, N_DIFF_HEADS, 2, HEAD_DIM)
        dv = proj[..., OFF_DV:OFF_NQ].reshape(B, S, N_DIFF_HEADS, DIFF_V_DIM)
        dq = partial_rope(rms_norm(dq, gq[0]), cos, sin)
        dk = partial_rope(rms_norm(dk, gq[1]), cos, sin)
        lq = lambda_qk[l].astype(jnp.float32)
        lam = jnp.exp(jnp.sum(lq[0] * lq[1])) - jnp.exp(jnp.sum(lq[2] * lq[3])) + lam_init
        o_diff = diff_attention(dq, dk, dv, lam, subln_gain[l], lam_init)
        nq = rms_norm(proj[..., OFF_NQ:OFF_NK].reshape(B, S, N_NA_HEADS, HEAD_DIM), gq[2])
        nk = rms_norm(proj[..., OFF_NK:OFF_NV].reshape(B, S, N_NA_HEADS, HEAD_DIM), gq[3])
        nv = proj[..., OFF_NV:OFF_MQ].reshape(B, S, N_NA_HEADS, HEAD_DIM)
        o_na = neighbourhood_attention(nq, nk, nv, na_bias[l])
        mq = rms_norm(proj[..., OFF_MQ:IN_WIDTH].reshape(B, S, N_MEM_HEADS, HEAD_DIM), gq[4])
        o_mem = memory_attention(mq, rms_norm(mem, mem_norm[l]), w_mem_kv[l], gq[5])
        mix = jnp.concatenate([o_diff, o_na, o_mem], axis=-1)
        x = x + mix @ w_out[l]
        x = x + peer_layer(rms_norm(x, ffn_norm[l]), w_pq[l], peer_keys[l], peer_u[l], peer_v[l])
    return x
```

```python
import functools
import math

import numpy as np
import jax
import jax.numpy as jnp
from jax import lax
from jax.experimental import pallas as pl
from jax.experimental.pallas import tpu as pltpu

F32 = jnp.float32
BF16 = jnp.bfloat16

HEAD_DIM = 128
N_DIFF_HEADS = 4
N_NA_HEADS = 4
N_MEM_HEADS = 4
DIFF_V_DIM = 2 * HEAD_DIM
DIFF_WIDTH = N_DIFF_HEADS * DIFF_V_DIM
NA_WIDTH = N_NA_HEADS * HEAD_DIM
MEM_WIDTH = N_MEM_HEADS * HEAD_DIM
IN_WIDTH = 3 * DIFF_WIDTH + 3 * NA_WIDTH + MEM_WIDTH
ROPE_THETA = 500000.0
ROT_DIM = HEAD_DIM // 4
ROT_HALF = ROT_DIM // 2
GRID_W = 64
MAX_WIN_H = 8
WIN_W = 16
PEER_HEADS = 8
N_KEYS = 128
PEER_TOPK = 16
EPS = 1e-6
ATTN_SCALE = HEAD_DIM ** -0.5
NEG_BIG = -1e30
SQRT_HALF = math.sqrt(0.5)

LANES = 128
VMEM_LIMIT = 56 * 1024 * 1024

COL_DQ, COL_DK, COL_DV = 0, 8, 16
COL_NQ, COL_NK, COL_NV, COL_MQ = 24, 28, 32, 36


def _params(*sem):
    return pltpu.CompilerParams(dimension_semantics=sem, vmem_limit_bytes=VMEM_LIMIT)


def _rms(p, g):
    ms = jnp.mean(p * p, axis=-1, keepdims=True)
    return p * lax.rsqrt(ms + EPS) * g


def _nt_dot(a, b):
    return lax.dot_general(a, b, (((1,), (1,)), ((), ())), preferred_element_type=F32)


IN_TN = 1024


def _inproj_body(x_ref, g_ref, w_ref, gq_ref, cf_ref, sa_ref, sb_ref, o_ref, h_ref):
    j = pl.program_id(1)

    @pl.when(j == 0)
    def _():
        h_ref[...] = _rms(x_ref[...], g_ref[...]).astype(BF16)

    acc = jnp.dot(h_ref[...], w_ref[...], preferred_element_type=F32)
    ngrp = IN_TN // HEAD_DIM

    def grp(g):
        return acc[:, g * HEAD_DIM:(g + 1) * HEAD_DIM]

    def put(g, val):
        o_ref[:, g * HEAD_DIM:(g + 1) * HEAD_DIM] = val.astype(BF16)

    def rope(p):
        return (p * cf_ref[...] + pltpu.roll(p, HEAD_DIM - ROT_HALF, 1) * sa_ref[...]
                + pltpu.roll(p, ROT_HALF, 1) * sb_ref[...])

    for jj in (0, 1):
        @pl.when(j == jj)
        def _(jj=jj):
            gain = gq_ref[jj:jj + 1, :]
            for g in range(ngrp):
                put(g, rope(_rms(grp(g), gain)))

    @pl.when(j == 2)
    def _():
        o_ref[...] = acc.astype(BF16)

    @pl.when(j == 3)
    def _():
        for g in range(ngrp):
            row = 2 if g < ngrp // 2 else 3
            put(g, _rms(grp(g), gq_ref[row:row + 1, :]))

    @pl.when(j == 4)
    def _():
        for g in range(ngrp):
            if g < ngrp // 2:
                put(g, grp(g))
            else:
                put(g, _rms(grp(g), gq_ref[4:5, :]))


def _in_proj(x2, g, w_bf, gq, cf, sa, sb, seq, tm=512):
    T, D = x2.shape
    nseq = seq // tm
    return pl.pallas_call(
        _inproj_body,
        grid=(T // tm, IN_WIDTH // IN_TN),
        in_specs=[
            pl.BlockSpec((tm, D), lambda i, j: (i, 0)),
            pl.BlockSpec((1, D), lambda i, j: (0, 0)),
            pl.BlockSpec((D, IN_TN), lambda i, j: (0, j)),
            pl.BlockSpec((6, HEAD_DIM), lambda i, j: (0, 0)),
            pl.BlockSpec((tm, HEAD_DIM), lambda i, j: (i % nseq, 0)),
            pl.BlockSpec((tm, HEAD_DIM), lambda i, j: (i % nseq, 0)),
            pl.BlockSpec((tm, HEAD_DIM), lambda i, j: (i % nseq, 0)),
        ],
        out_specs=pl.BlockSpec((tm, IN_TN), lambda i, j: (i, j)),
        out_shape=jax.ShapeDtypeStruct((T, IN_WIDTH), BF16),
        scratch_shapes=[pltpu.VMEM((tm, D), BF16)],
        compiler_params=_params("parallel", "arbitrary"),
        name="in_proj",
    )(x2, g, w_bf, gq, cf, sa, sb)


def _diff_body(lq_ref, g_ref, q_ref, k_ref, v_ref, o_ref, *, lam_init):
    lq = lq_ref[...]
    lam = (jnp.exp(jnp.sum(lq[0:1] * lq[1:2], axis=-1, keepdims=True))
           - jnp.exp(jnp.sum(lq[2:3] * lq[3:4], axis=-1, keepdims=True)) + lam_init)

    def softmax_map(c):
        q = q_ref[:, c * HEAD_DIM:(c + 1) * HEAD_DIM]
        k = k_ref[:, c * HEAD_DIM:(c + 1) * HEAD_DIM]
        s = _nt_dot(q, k) * ATTN_SCALE
        e = jnp.exp(s - jnp.max(s, axis=-1, keepdims=True))
        return e * (1.0 / jnp.sum(e, axis=-1, keepdims=True))

    pd = softmax_map(0) - lam * softmax_map(1)
    o = jnp.dot(pd.astype(BF16), v_ref[...], preferred_element_type=F32)
    o_ref[...] = (_rms(o, g_ref[...]) * (1.0 - lam_init)).astype(o_ref.dtype)


def _diff_attention(proj3, lq, sub_g, lam_init, tq=256):
    B, S, _ = proj3.shape
    return pl.pallas_call(
        functools.partial(_diff_body, lam_init=lam_init),
        grid=(B, N_DIFF_HEADS, S // tq),
        in_specs=[
            pl.BlockSpec((4, HEAD_DIM), lambda b, h, i: (0, 0)),
            pl.BlockSpec((1, DIFF_V_DIM), lambda b, h, i: (0, 0)),
            pl.BlockSpec((None, tq, DIFF_V_DIM), lambda b, h, i: (b, i, COL_DQ // 2 + h)),
            pl.BlockSpec((None, S, DIFF_V_DIM), lambda b, h, i: (b, 0, COL_DK // 2 + h)),
            pl.BlockSpec((None, S, DIFF_V_DIM), lambda b, h, i: (b, 0, COL_DV // 2 + h)),
        ],
        out_specs=pl.BlockSpec((None, tq, DIFF_V_DIM), lambda b, h, i: (b, i, h)),
        out_shape=jax.ShapeDtypeStruct((B, S, DIFF_WIDTH), BF16),
        compiler_params=_params("parallel", "parallel", "arbitrary"),
        name="diff_attn",
    )(lq, sub_g, proj3, proj3, proj3)


NA_QROWS = 8
NA_KROWS = 16
NA_TQ = NA_QROWS * GRID_W
NA_TK = NA_KROWS * GRID_W


def _na_bias_index(rows):
    qr = np.arange(NA_QROWS)[:, None, None, None]
    qc = np.arange(GRID_W)[None, :, None, None]
    kr = np.arange(NA_KROWS)[None, None, :, None]
    kc = np.arange(GRID_W)[None, None, None, :]
    shape = (NA_QROWS, GRID_W, NA_KROWS, GRID_W)
    drows, dcols, valids = [], [], []
    for r0, kb in ((0, 0), (NA_QROWS, NA_QROWS - MAX_WIN_H // 2), (rows - NA_QROWS, rows - NA_KROWS)):
        r = r0 + qr
        rs = np.clip(r - MAX_WIN_H // 2, 0, rows - MAX_WIN_H)
        ka = kb + kr
        vr = (ka >= rs) & (ka < rs + MAX_WIN_H)
        drow = ka - r + (MAX_WIN_H - 1)
        cs = np.clip(qc - WIN_W // 2, 0, GRID_W - WIN_W)
        vc = (kc >= cs) & (kc < cs + WIN_W)
        dcol = kc - qc + (WIN_W - 1)
        valids.append(np.broadcast_to(vr & vc, shape).reshape(NA_TQ, NA_TK))
        drows.append(np.broadcast_to(np.clip(drow, 0, 2 * MAX_WIN_H - 2), shape).reshape(NA_TQ, NA_TK))
        dcols.append(np.broadcast_to(np.clip(dcol, 0, 2 * WIN_W - 2), shape).reshape(NA_TQ, NA_TK))
    return np.stack(drows), np.stack(dcols), np.stack(valids)


def _na_body(q_ref, k_ref, v_ref, b_ref, o_ref, *, rows):
    i = pl.program_id(2)
    kb = jnp.clip(i * NA_QROWS - MAX_WIN_H // 2, 0, rows - NA_KROWS)
    start = pl.multiple_of(kb * GRID_W, GRID_W)
    k = k_ref[pl.ds(start, NA_TK), :]
    v = v_ref[pl.ds(start, NA_TK), :]
    s = _nt_dot(q_ref[...], k) * ATTN_SCALE + b_ref[...]
    e = jnp.exp(s - jnp.max(s, axis=-1, keepdims=True))
    p = e * (1.0 / jnp.sum(e, axis=-1, keepdims=True))
    o_ref[...] = jnp.dot(p.astype(BF16), v, preferred_element_type=F32).astype(o_ref.dtype)


def _na_attention(proj3, bias_tab):
    B, S, _ = proj3.shape
    rows = S // GRID_W
    nblk = rows // NA_QROWS

    def bias_map(b, h, i):
        return (h, jnp.where(i == 0, 0, jnp.where(i == nblk - 1, 2, 1)), 0, 0)

    return pl.pallas_call(
        functools.partial(_na_body, rows=rows),
        grid=(B, N_NA_HEADS, nblk),
        in_specs=[
            pl.BlockSpec((None, NA_TQ, HEAD_DIM), lambda b, h, i: (b, i, COL_NQ + h)),
            pl.BlockSpec((None, S, HEAD_DIM), lambda b, h, i: (b, 0, COL_NK + h)),
            pl.BlockSpec((None, S, HEAD_DIM), lambda b, h, i: (b, 0, COL_NV + h)),
            pl.BlockSpec((None, None, NA_TQ, NA_TK), bias_map),
        ],
        out_specs=pl.BlockSpec((None, NA_TQ, HEAD_DIM), lambda b, h, i: (b, i, h)),
        out_shape=jax.ShapeDtypeStruct((B, S, NA_WIDTH), BF16),
        compiler_params=_params("parallel", "parallel", "arbitrary"),
        name="na_attn",
    )(proj3, proj3, proj3, bias_tab)


def _memkv_body(m_ref, g_ref, w_ref, gk_ref, k_ref, v_ref):
    hm = _rms(m_ref[...], g_ref[...]).astype(BF16)
    kv = jnp.dot(hm, w_ref[...], preferred_element_type=F32)
    for h in range(N_MEM_HEADS):
        sl = slice(h * HEAD_DIM, (h + 1) * HEAD_DIM)
        k_ref[:, sl] = _rms(kv[:, sl], gk_ref[...]).astype(BF16)
    v_ref[...] = kv[:, MEM_WIDTH:].astype(BF16)


def _mem_kv(mem, g, w_bf, gk):
    B, M, D = mem.shape
    return pl.pallas_call(
        _memkv_body,
        grid=(B,),
        in_specs=[
            pl.BlockSpec((None, M, D), lambda b: (b, 0, 0)),
            pl.BlockSpec((1, D), lambda b: (0, 0)),
            pl.BlockSpec((D, 2 * MEM_WIDTH), lambda b: (0, 0)),
            pl.BlockSpec((1, HEAD_DIM), lambda b: (0, 0)),
        ],
        out_specs=[pl.BlockSpec((None, M, MEM_WIDTH), lambda b: (b, 0, 0))] * 2,
        out_shape=[jax.ShapeDtypeStruct((B, M, MEM_WIDTH), BF16)] * 2,
        compiler_params=_params("parallel"),
        name="mem_kv",
    )(mem, g, w_bf, gk)


def _memattn_body(q_ref, k_ref, v_ref, o_ref):
    for h in range(N_MEM_HEADS):
        sl = slice(h * HEAD_DIM, (h + 1) * HEAD_DIM)
        s = _nt_dot(q_ref[:, sl], k_ref[:, sl]) * ATTN_SCALE
        e = jnp.exp(s - jnp.max(s, axis=-1, keepdims=True))
        p = e * (1.0 / jnp.sum(e, axis=-1, keepdims=True))
        o_ref[:, sl] = jnp.dot(p.astype(BF16), v_ref[:, sl], preferred_element_type=F32).astype(o_ref.dtype)


def _mem_attention(proj3, mk, mv, tq=512):
    B, S, _ = proj3.shape
    M = mk.shape[1]
    return pl.pallas_call(
        _memattn_body,
        grid=(B, S // tq),
        in_specs=[
            pl.BlockSpec((None, tq, MEM_WIDTH), lambda b, i: (b, i, COL_MQ // N_MEM_HEADS)),
            pl.BlockSpec((None, M, MEM_WIDTH), lambda b, i: (b, 0, 0)),
            pl.BlockSpec((None, M, MEM_WIDTH), lambda b, i: (b, 0, 0)),
        ],
        out_specs=pl.BlockSpec((None, tq, MEM_WIDTH), lambda b, i: (b, i, 0)),
        out_shape=jax.ShapeDtypeStruct((B, S, MEM_WIDTH), BF16),
        compiler_params=_params("parallel", "arbitrary"),
        name="mem_attn",
    )(proj3, mk, mv)


def _outproj_body(x_ref, od_ref, on_ref, om_ref, w_ref, g_ref, xn_ref, h_ref):
    mix = jnp.concatenate([od_ref[...], on_ref[...], om_ref[...]], axis=-1)
    xn = x_ref[...] + jnp.dot(mix, w_ref[...], preferred_element_type=F32)
    xn_ref[...] = xn
    h_ref[...] = _rms(xn, g_ref[...]).astype(BF16)


def _out_proj(x2, od, on, om, w_bf, g, tm=256):
    T, D = x2.shape
    return pl.pallas_call(
        _outproj_body,
        grid=(T // tm,),
        in_specs=[
            pl.BlockSpec((tm, D), lambda i: (i, 0)),
            pl.BlockSpec((tm, DIFF_WIDTH), lambda i: (i, 0)),
            pl.BlockSpec((tm, NA_WIDTH), lambda i: (i, 0)),
            pl.BlockSpec((tm, MEM_WIDTH), lambda i: (i, 0)),
            pl.BlockSpec((D, D), lambda i: (0, 0)),
            pl.BlockSpec((1, D), lambda i: (0, 0)),
        ],
        out_specs=[pl.BlockSpec((tm, D), lambda i: (i, 0))] * 2,
        out_shape=[jax.ShapeDtypeStruct((T, D), F32), jax.ShapeDtypeStruct((T, D), BF16)],
        compiler_params=_params("parallel"),
        name="out_proj",
    )(x2, od, on, om, w_bf, g)


def _top_values(x, k):
    n = x.shape[0]
    iota = lax.broadcasted_iota(jnp.int32, x.shape, 0)
    vals = []
    for _ in range(k):
        m = jnp.max(x, axis=0, keepdims=True)
        vals.append(m)
        first = jnp.min(jnp.where(x == m, iota, n), axis=0, keepdims=True)
        x = jnp.where(iota == first, -jnp.inf, x)
    return vals


def _router_body(h_ref, w_ref, keys_ref, s1_ref, s2_ref, tau_ref, mlz_ref):
    q = jnp.dot(h_ref[...], w_ref[...], preferred_element_type=F32).astype(BF16)
    for h in range(PEER_HEADS):
        tops = []
        for c, s_ref in ((0, s1_ref), (1, s2_ref)):
            col = (2 * h + c) * N_KEYS
            s = _nt_dot(keys_ref[h, c], q[:, col:col + N_KEYS])
            s_ref[h] = s
            tops.append(_top_values(s, PEER_TOPK))
        b = jnp.concatenate(tops[1], axis=0)
        cand = jnp.concatenate([a + b for a in tops[0]], axis=0)
        best = _top_values(cand, PEER_TOPK)
        z = best[0] * 0.0
        for v in best:
            z = z + jnp.exp(v - best[0])
        tau_ref[h:h + 1, :] = best[PEER_TOPK - 1]
        mlz_ref[h:h + 1, :] = best[0] + jnp.log(z)


def _router(h2, w_bf, keys_bf, tr=256):
    T, D = h2.shape
    qw = w_bf.shape[1]
    return pl.pallas_call(
        _router_body,
        grid=(T // tr,),
        in_specs=[
            pl.BlockSpec((tr, D), lambda i: (i, 0)),
            pl.BlockSpec((D, qw), lambda i: (0, 0)),
            pl.BlockSpec((PEER_HEADS, 2, N_KEYS, N_KEYS), lambda i: (0, 0, 0, 0)),
        ],
        out_specs=[
            pl.BlockSpec((PEER_HEADS, N_KEYS, tr), lambda i: (0, 0, i)),
            pl.BlockSpec((PEER_HEADS, N_KEYS, tr), lambda i: (0, 0, i)),
            pl.BlockSpec((PEER_HEADS, tr), lambda i: (0, i)),
            pl.BlockSpec((PEER_HEADS, tr), lambda i: (0, i)),
        ],
        out_shape=[
            jax.ShapeDtypeStruct((PEER_HEADS, N_KEYS, T), F32),
            jax.ShapeDtypeStruct((PEER_HEADS, N_KEYS, T), F32),
            jax.ShapeDtypeStruct((PEER_HEADS, T), F32),
            jax.ShapeDtypeStruct((PEER_HEADS, T), F32),
        ],
        compiler_params=_params("parallel"),
        name="peer_router",
    )(h2, w_bf, keys_bf)


PEER_TC = 128
PEER_SUB = 512


def _peer_body(h_ref, u_ref, vt_ref, s1_ref, s2_ref, tau_ref, mlz_ref, o_ref, a_ref, g_ref):
    j = pl.program_id(1)
    te = u_ref.shape[0]
    tt = h_ref.shape[0]

    @pl.when(j == 0)
    def _():
        o_ref[...] = jnp.zeros_like(o_ref)

    for sub in range(te // PEER_SUB):
        erows = slice(sub * PEER_SUB, (sub + 1) * PEER_SUB)
        a_ref[...] = _nt_dot(u_ref[erows, :], h_ref[...])
        for il in range(PEER_SUB // N_KEYS):
            i1 = sub * (PEER_SUB // N_KEYS) + il
            rows = slice(il * N_KEYS, (il + 1) * N_KEYS)
            for tc in range(tt // PEER_TC):
                lanes = slice(tc * PEER_TC, (tc + 1) * PEER_TC)
                w = jnp.zeros((N_KEYS, PEER_TC), F32)
                for h in range(PEER_HEADS):
                    z = s2_ref[h, :, lanes] + s1_ref[h, i1:i1 + 1, lanes]
                    w = w + jnp.where(z >= tau_ref[h:h + 1, lanes],
                                      jnp.exp(z - mlz_ref[h:h + 1, lanes]), 0.0)
                a = a_ref[rows, lanes]
                gelu = 0.5 * a * (1.0 + lax.erf(a * SQRT_HALF))
                g_ref[rows, lanes] = (w * gelu).astype(BF16)
        o_ref[...] += jnp.dot(vt_ref[:, erows], g_ref[...], preferred_element_type=F32)


def _peer_dense(h2, u_bf, vt_bf, s1, s2, tau, mlz, tt=512, te=1024):
    T, D = h2.shape
    E = u_bf.shape[0]
    n_i1 = te // N_KEYS
    return pl.pallas_call(
        _peer_body,
        grid=(T // tt, E // te),
        in_specs=[
            pl.BlockSpec((tt, D), lambda i, j: (i, 0)),
            pl.BlockSpec((te, D), lambda i, j: (j, 0)),
            pl.BlockSpec((D, te), lambda i, j: (0, j)),
            pl.BlockSpec((PEER_HEADS, n_i1, tt), lambda i, j: (0, j, i)),
            pl.BlockSpec((PEER_HEADS, N_KEYS, tt), lambda i, j: (0, 0, i)),
            pl.BlockSpec((PEER_HEADS, tt), lambda i, j: (0, i)),
            pl.BlockSpec((PEER_HEADS, tt), lambda i, j: (0, i)),
        ],
        out_specs=pl.BlockSpec((D, tt), lambda i, j: (0, i)),
        out_shape=jax.ShapeDtypeStruct((D, T), F32),
        scratch_shapes=[pltpu.VMEM((PEER_SUB, tt), F32), pltpu.VMEM((PEER_SUB, tt), BF16)],
        compiler_params=_params("parallel", "arbitrary"),
        name="peer_dense",
    )(h2, u_bf, vt_bf, s1, s2, tau, mlz)


def _resid_body(x_ref, yt_ref, o_ref):
    o_ref[...] = x_ref[...] + yt_ref[...].T


def _residual_t(xn, yt, tm=512):
    T, D = xn.shape
    return pl.pallas_call(
        _resid_body,
        grid=(T // tm,),
        in_specs=[pl.BlockSpec((tm, D), lambda i: (i, 0)), pl.BlockSpec((D, tm), lambda i: (0, i))],
        out_specs=pl.BlockSpec((tm, D), lambda i: (i, 0)),
        out_shape=jax.ShapeDtypeStruct((T, D), F32),
        compiler_params=_params("parallel"),
        name="peer_residual",
    )(xn, yt)


def _rope_tables(seq):
    inv = ROPE_THETA ** (-jnp.arange(0, ROT_DIM, 2, dtype=F32) / ROT_DIM)
    ang = jnp.arange(seq, dtype=F32)[:, None] * inv[None, :]
    cos, sin = jnp.cos(ang), jnp.sin(ang)
    rest = HEAD_DIM - ROT_DIM
    zeros_h = jnp.zeros((seq, ROT_HALF), F32)
    cf = jnp.concatenate([cos, cos, jnp.ones((seq, rest), F32)], axis=-1)
    sa = jnp.concatenate([-sin, zeros_h, jnp.zeros((seq, rest), F32)], axis=-1)
    sb = jnp.concatenate([zeros_h, sin, jnp.zeros((seq, rest), F32)], axis=-1)
    return cf, sa, sb


def kernel(x, mem, attn_norm, w_in, qk_gain, lambda_qk, subln_gain, na_bias, mem_norm, w_mem_kv,
           w_out, ffn_norm, w_pq, peer_keys, peer_u, peer_v):
    B, S, D = x.shape
    depth = w_in.shape[0]
    T = B * S
    cf, sa, sb = _rope_tables(S)
    drow, dcol, valid = _na_bias_index(S // GRID_W)
    x2 = x.reshape(T, D)
    for l in range(depth):
        lam_init = 0.8 - 0.6 * math.exp(-0.3 * l)
        proj = _in_proj(x2, attn_norm[l][None], w_in[l].astype(BF16), qk_gain[l], cf, sa, sb, S)
        proj3 = proj.reshape(B, S, IN_WIDTH)
        o_diff = _diff_attention(proj3, lambda_qk[l], subln_gain[l][None], lam_init)
        bias_tab = jnp.where(valid[None], na_bias[l][:, drow, dcol], NEG_BIG)
        o_na = _na_attention(proj3, bias_tab)
        mk, mv = _mem_kv(mem, mem_norm[l][None], w_mem_kv[l].astype(BF16), qk_gain[l][5:6])
        o_mem = _mem_attention(proj3, mk, mv)
        xn, h2 = _out_proj(x2, o_diff.reshape(T, DIFF_WIDTH), o_na.reshape(T, NA_WIDTH),
                           o_mem.reshape(T, MEM_WIDTH), w_out[l].astype(BF16), ffn_norm[l][None])
        s1, s2, tau, mlz = _router(h2, w_pq[l].astype(BF16), peer_keys[l].astype(BF16))
        yt = _peer_dense(h2, peer_u[l].astype(BF16), peer_v[l].T.astype(BF16), s1, s2, tau, mlz)
        x2 = _residual_t(xn, yt)
    return x2.reshape(B, S, D)
```

```python
import functools
import math

import jax
import jax.numpy as jnp
from jax import lax
from jax.experimental import pallas as pl
from jax.experimental.pallas import tpu as pltpu

F32 = jnp.float32
BF16 = jnp.bfloat16

HEAD_DIM = 128
N_DIFF_HEADS = 4
N_NA_HEADS = 4
N_MEM_HEADS = 4
DIFF_V_DIM = 2 * HEAD_DIM
DIFF_WIDTH = N_DIFF_HEADS * DIFF_V_DIM
NA_WIDTH = N_NA_HEADS * HEAD_DIM
MEM_WIDTH = N_MEM_HEADS * HEAD_DIM
IN_WIDTH = 3 * DIFF_WIDTH + 3 * NA_WIDTH + MEM_WIDTH
ROPE_THETA = 500000.0
ROT_DIM = HEAD_DIM // 4
ROT_HALF = ROT_DIM // 2
GRID_W = 64
MAX_WIN_H = 8
WIN_W = 16
PEER_HEADS = 8
N_KEYS = 128
PEER_TOPK = 16
EPS = 1e-6
ATTN_SCALE = HEAD_DIM ** -0.5
NEG_BIG = -1e30
SQRT_HALF = math.sqrt(0.5)

LANES = 128
VMEM_LIMIT = 56 * 1024 * 1024

COL_DQ, COL_DK, COL_DV = 0, 8, 16
COL_NQ, COL_NK, COL_NV, COL_MQ = 24, 28, 32, 36


def _params(*sem):
    return pltpu.CompilerParams(dimension_semantics=sem, vmem_limit_bytes=VMEM_LIMIT)


def _rms(p, g):
    ms = jnp.mean(p * p, axis=-1, keepdims=True)
    return p * lax.rsqrt(ms + EPS) * g


def _nt_dot(a, b):
    return lax.dot_general(a, b, (((1,), (1,)), ((), ())), preferred_element_type=F32)


IN_TN = 1024


def _inproj_body(x_ref, g_ref, w_ref, gq_ref, cf_ref, sa_ref, sb_ref, o_ref, h_ref):
    j = pl.program_id(1)

    @pl.when(j == 0)
    def _():
        h_ref[...] = _rms(x_ref[...], g_ref[...]).astype(BF16)

    acc = jnp.dot(h_ref[...], w_ref[...], preferred_element_type=F32)
    ngrp = IN_TN // HEAD_DIM

    def grp(g):
        return acc[:, g * HEAD_DIM:(g + 1) * HEAD_DIM]

    def put(g, val):
        o_ref[:, g * HEAD_DIM:(g + 1) * HEAD_DIM] = val.astype(BF16)

    def rope(p):
        return (p * cf_ref[...] + pltpu.roll(p, HEAD_DIM - ROT_HALF, 1) * sa_ref[...]
                + pltpu.roll(p, ROT_HALF, 1) * sb_ref[...])

    for jj in (0, 1):
        @pl.when(j == jj)
        def _(jj=jj):
            gain = gq_ref[jj:jj + 1, :]
            for g in range(ngrp):
                put(g, rope(_rms(grp(g), gain)))

    @pl.when(j == 2)
    def _():
        o_ref[...] = acc.astype(BF16)

    @pl.when(j == 3)
    def _():
        for g in range(ngrp):
            row = 2 if g < ngrp // 2 else 3
            put(g, _rms(grp(g), gq_ref[row:row + 1, :]))

    @pl.when(j == 4)
    def _():
        for g in range(ngrp):
            if g < ngrp // 2:
                put(g, grp(g))
            else:
                put(g, _rms(grp(g), gq_ref[4:5, :]))


def _in_proj(x2, g, w_bf, gq, cf, sa, sb, seq, tm=512):
    T, D = x2.shape
    nseq = seq // tm
    return pl.pallas_call(
        _inproj_body,
        grid=(T // tm, IN_WIDTH // IN_TN),
        in_specs=[
            pl.BlockSpec((tm, D), lambda i, j: (i, 0)),
            pl.BlockSpec((1, D), lambda i, j: (0, 0)),
            pl.BlockSpec((D, IN_TN), lambda i, j: (0, j)),
            pl.BlockSpec((6, HEAD_DIM), lambda i, j: (0, 0)),
            pl.BlockSpec((tm, HEAD_DIM), lambda i, j: (i % nseq, 0)),
            pl.BlockSpec((tm, HEAD_DIM), lambda i, j: (i % nseq, 0)),
            pl.BlockSpec((tm, HEAD_DIM), lambda i, j: (i % nseq, 0)),
        ],
        out_specs=pl.BlockSpec((tm, IN_TN), lambda i, j: (i, j)),
        out_shape=jax.ShapeDtypeStruct((T, IN_WIDTH), BF16),
        scratch_shapes=[pltpu.VMEM((tm, D), BF16)],
        compiler_params=_params("parallel", "arbitrary"),
        name="in_proj",
    )(x2, g, w_bf, gq, cf, sa, sb)


def _diff_body(lq_ref, g_ref, q_ref, k_ref, v_ref, o_ref, *, lam_init):
    lq = lq_ref[...]
    lam = (jnp.exp(jnp.sum(lq[0:1] * lq[1:2], axis=-1, keepdims=True))
           - jnp.exp(jnp.sum(lq[2:3] * lq[3:4], axis=-1, keepdims=True)) + lam_init)

    def softmax_map(c):
        q = q_ref[:, c * HEAD_DIM:(c + 1) * HEAD_DIM]
        k = k_ref[:, c * HEAD_DIM:(c + 1) * HEAD_DIM]
        s = _nt_dot(q, k) * ATTN_SCALE
        e = jnp.exp(s - jnp.max(s, axis=-1, keepdims=True))
        return e * (1.0 / jnp.sum(e, axis=-1, keepdims=True))

    pd = softmax_map(0) - lam * softmax_map(1)
    o = jnp.dot(pd.astype(BF16), v_ref[...], preferred_element_type=F32)
    o_ref[...] = (_rms(o, g_ref[...]) * (1.0 - lam_init)).astype(o_ref.dtype)


def _diff_attention(proj3, lq, sub_g, lam_init, tq=256):
    B, S, _ = proj3.shape
    return pl.pallas_call(
        functools.partial(_diff_body, lam_init=lam_init),
        grid=(B, N_DIFF_HEADS, S // tq),
        in_specs=[
            pl.BlockSpec((4, HEAD_DIM), lambda b, h, i: (0, 0)),
            pl.BlockSpec((1, DIFF_V_DIM), lambda b, h, i: (0, 0)),
            pl.BlockSpec((None, tq, DIFF_V_DIM), lambda b, h, i: (b, i, COL_DQ // 2 + h)),
            pl.BlockSpec((None, S, DIFF_V_DIM), lambda b, h, i: (b, 0, COL_DK // 2 + h)),
            pl.BlockSpec((None, S, DIFF_V_DIM), lambda b, h, i: (b, 0, COL_DV // 2 + h)),
        ],
        out_specs=pl.BlockSpec((None, tq, DIFF_V_DIM), lambda b, h, i: (b, i, h)),
        out_shape=jax.ShapeDtypeStruct((B, S, DIFF_WIDTH), BF16),
        compiler_params=_params("parallel", "parallel", "arbitrary"),
        name="diff_attn",
    )(lq, sub_g, proj3, proj3, proj3)


NA_QROWS = 8
NA_KROWS = 16
NA_TQ = NA_QROWS * GRID_W
NA_TK = NA_KROWS * GRID_W


def _na_table_body(b_ref, o_ref, *, rows):
    h = pl.program_id(0)
    variant = pl.program_id(1)
    qc = lax.broadcasted_iota(jnp.int32, (GRID_W, GRID_W), 0)
    kc = lax.broadcasted_iota(jnp.int32, (GRID_W, GRID_W), 1)
    cs = jnp.clip(qc - WIN_W // 2, 0, GRID_W - WIN_W)
    col_ok = (kc >= cs) & (kc < cs + WIN_W)
    dcol = kc - qc + (WIN_W - 1)
    masked = jnp.full((GRID_W, GRID_W), NEG_BIG, F32)
    col_tab = []
    for dr in range(2 * MAX_WIN_H - 1):
        t = masked
        for dc in range(2 * WIN_W - 1):
            t = jnp.where(dcol == dc, b_ref[h, dr, dc], t)
        col_tab.append(jnp.where(col_ok, t, NEG_BIG))

    blocks = ((0, 0), (NA_QROWS, NA_QROWS - MAX_WIN_H // 2), (rows - NA_QROWS, rows - NA_KROWS))
    for vi, (r0, kb) in enumerate(blocks):
        @pl.when(variant == vi)
        def _(r0=r0, kb=kb):
            for qr in range(NA_QROWS):
                r = r0 + qr
                rs = min(max(r - MAX_WIN_H // 2, 0), rows - MAX_WIN_H)
                for kp in range(NA_KROWS // 2):
                    halves = []
                    for kr in (2 * kp, 2 * kp + 1):
                        ka = kb + kr
                        inside = rs <= ka < rs + MAX_WIN_H
                        halves.append(col_tab[ka - r + MAX_WIN_H - 1] if inside else masked)
                    o_ref[qr * GRID_W:(qr + 1) * GRID_W, kp * LANES:(kp + 1) * LANES] = (
                        jnp.concatenate(halves, axis=1))


def _na_bias_table(bias, rows):
    nh = bias.shape[0]
    return pl.pallas_call(
        functools.partial(_na_table_body, rows=rows),
        grid=(nh, 3),
        in_specs=[pl.BlockSpec(memory_space=pltpu.SMEM)],
        out_specs=pl.BlockSpec((None, None, NA_TQ, NA_TK), lambda h, v: (h, v, 0, 0)),
        out_shape=jax.ShapeDtypeStruct((nh, 3, NA_TQ, NA_TK), F32),
        compiler_params=_params("parallel", "arbitrary"),
        name="na_bias_table",
    )(bias)


def _na_body(q_ref, k_ref, v_ref, b_ref, o_ref, *, rows):
    i = pl.program_id(2)
    kb = jnp.clip(i * NA_QROWS - MAX_WIN_H // 2, 0, rows - NA_KROWS)
    start = pl.multiple_of(kb * GRID_W, GRID_W)
    k = k_ref[pl.ds(start, NA_TK), :]
    v = v_ref[pl.ds(start, NA_TK), :]
    s = _nt_dot(q_ref[...], k) * ATTN_SCALE + b_ref[...]
    e = jnp.exp(s - jnp.max(s, axis=-1, keepdims=True))
    p = e * (1.0 / jnp.sum(e, axis=-1, keepdims=True))
    o_ref[...] = jnp.dot(p.astype(BF16), v, preferred_element_type=F32).astype(o_ref.dtype)


def _na_attention(proj3, bias_tab):
    B, S, _ = proj3.shape
    rows = S // GRID_W
    nblk = rows // NA_QROWS

    def bias_map(b, h, i):
        return (h, jnp.where(i == 0, 0, jnp.where(i == nblk - 1, 2, 1)), 0, 0)

    return pl.pallas_call(
        functools.partial(_na_body, rows=rows),
        grid=(B, N_NA_HEADS, nblk),
        in_specs=[
            pl.BlockSpec((None, NA_TQ, HEAD_DIM), lambda b, h, i: (b, i, COL_NQ + h)),
            pl.BlockSpec((None, S, HEAD_DIM), lambda b, h, i: (b, 0, COL_NK + h)),
            pl.BlockSpec((None, S, HEAD_DIM), lambda b, h, i: (b, 0, COL_NV + h)),
            pl.BlockSpec((None, None, NA_TQ, NA_TK), bias_map),
        ],
        out_specs=pl.BlockSpec((None, NA_TQ, HEAD_DIM), lambda b, h, i: (b, i, h)),
        out_shape=jax.ShapeDtypeStruct((B, S, NA_WIDTH), BF16),
        compiler_params=_params("parallel", "parallel", "arbitrary"),
        name="na_attn",
    )(proj3, proj3, proj3, bias_tab)


def _memkv_body(m_ref, g_ref, w_ref, gk_ref, k_ref, v_ref):
    hm = _rms(m_ref[...], g_ref[...]).astype(BF16)
    kv = jnp.dot(hm, w_ref[...], preferred_element_type=F32)
    for h in range(N_MEM_HEADS):
        sl = slice(h * HEAD_DIM, (h + 1) * HEAD_DIM)
        k_ref[:, sl] = _rms(kv[:, sl], gk_ref[...]).astype(BF16)
    v_ref[...] = kv[:, MEM_WIDTH:].astype(BF16)


def _mem_kv(mem, g, w_bf, gk):
    B, M, D = mem.shape
    return pl.pallas_call(
        _memkv_body,
        grid=(B,),
        in_specs=[
            pl.BlockSpec((None, M, D), lambda b: (b, 0, 0)),
            pl.BlockSpec((1, D), lambda b: (0, 0)),
            pl.BlockSpec((D, 2 * MEM_WIDTH), lambda b: (0, 0)),
            pl.BlockSpec((1, HEAD_DIM), lambda b: (0, 0)),
        ],
        out_specs=[pl.BlockSpec((None, M, MEM_WIDTH), lambda b: (b, 0, 0))] * 2,
        out_shape=[jax.ShapeDtypeStruct((B, M, MEM_WIDTH), BF16)] * 2,
        compiler_params=_params("parallel"),
        name="mem_kv",
    )(mem, g, w_bf, gk)


def _memattn_body(q_ref, k_ref, v_ref, o_ref):
    for h in range(N_MEM_HEADS):
        sl = slice(h * HEAD_DIM, (h + 1) * HEAD_DIM)
        s = _nt_dot(q_ref[:, sl], k_ref[:, sl]) * ATTN_SCALE
        e = jnp.exp(s - jnp.max(s, axis=-1, keepdims=True))
        p = e * (1.0 / jnp.sum(e, axis=-1, keepdims=True))
        o_ref[:, sl] = jnp.dot(p.astype(BF16), v_ref[:, sl], preferred_element_type=F32).astype(o_ref.dtype)


def _mem_attention(proj3, mk, mv, tq=512):
    B, S, _ = proj3.shape
    M = mk.shape[1]
    return pl.pallas_call(
        _memattn_body,
        grid=(B, S // tq),
        in_specs=[
            pl.BlockSpec((None, tq, MEM_WIDTH), lambda b, i: (b, i, COL_MQ // N_MEM_HEADS)),
            pl.BlockSpec((None, M, MEM_WIDTH), lambda b, i: (b, 0, 0)),
            pl.BlockSpec((None, M, MEM_WIDTH), lambda b, i: (b, 0, 0)),
        ],
        out_specs=pl.BlockSpec((None, tq, MEM_WIDTH), lambda b, i: (b, i, 0)),
        out_shape=jax.ShapeDtypeStruct((B, S, MEM_WIDTH), BF16),
        compiler_params=_params("parallel", "arbitrary"),
        name="mem_attn",
    )(proj3, mk, mv)


def _outproj_body(x_ref, od_ref, on_ref, om_ref, w_ref, g_ref, xn_ref, h_ref, ht_ref):
    mix = jnp.concatenate([od_ref[...], on_ref[...], om_ref[...]], axis=-1)
    xn = x_ref[...] + jnp.dot(mix, w_ref[...], preferred_element_type=F32)
    xn_ref[...] = xn
    h = _rms(xn, g_ref[...])
    h_ref[...] = h.astype(BF16)
    ht_ref[...] = h.T.astype(BF16)


def _out_proj(x2, od, on, om, w_bf, g, tm=256):
    T, D = x2.shape
    return pl.pallas_call(
        _outproj_body,
        grid=(T // tm,),
        in_specs=[
            pl.BlockSpec((tm, D), lambda i: (i, 0)),
            pl.BlockSpec((tm, DIFF_WIDTH), lambda i: (i, 0)),
            pl.BlockSpec((tm, NA_WIDTH), lambda i: (i, 0)),
            pl.BlockSpec((tm, MEM_WIDTH), lambda i: (i, 0)),
            pl.BlockSpec((D, D), lambda i: (0, 0)),
            pl.BlockSpec((1, D), lambda i: (0, 0)),
        ],
        out_specs=[pl.BlockSpec((tm, D), lambda i: (i, 0))] * 2 + [pl.BlockSpec((D, tm), lambda i: (0, i))],
        out_shape=[jax.ShapeDtypeStruct((T, D), F32), jax.ShapeDtypeStruct((T, D), BF16),
                   jax.ShapeDtypeStruct((D, T), BF16)],
        compiler_params=_params("parallel"),
        name="out_proj",
    )(x2, od, on, om, w_bf, g)


LOG2E = math.log2(math.e)
KH = N_KEYS * PEER_HEADS


def _sort_pairs(n):
    pairs = []

    def merge(lo, hi, r):
        step = r * 2
        if step < hi - lo:
            merge(lo, hi, step)
            merge(lo + r, hi, step)
            pairs.extend((i, i + r) for i in range(lo + r, hi - r, step))
        else:
            pairs.append((lo, lo + r))

    def sort(lo, hi):
        if hi > lo:
            mid = lo + (hi - lo) // 2
            sort(lo, mid)
            sort(mid + 1, hi)
            merge(lo, hi, 1)

    sort(0, n - 1)
    return pairs


_SORT16 = _sort_pairs(PEER_TOPK)


def _exchange(v, i, j):
    v[i], v[j] = jnp.maximum(v[i], v[j]), jnp.minimum(v[i], v[j])


def _sorted_desc(v):
    v = list(v)
    for i, j in _SORT16:
        _exchange(v, i, j)
    return v


def _merge_top(a, b):
    k = PEER_TOPK
    c = list(a)
    for i in range(len(b)):
        c[k - 1 - i] = jnp.maximum(a[k - 1 - i], b[i])
    d = k // 2
    while d >= 1:
        for i in range(k):
            if i & d == 0:
                _exchange(c, i, i + d)
        d //= 2
    return c


def _top16_of_keys(s_ref):
    best = None
    for g in range(N_KEYS // PEER_TOPK):
        grp = _sorted_desc([s_ref[(g * PEER_TOPK + r) * PEER_HEADS:(g * PEER_TOPK + r + 1) * PEER_HEADS, :]
                            for r in range(PEER_TOPK)])
        best = grp if best is None else _merge_top(best, grp)
    return best


def _router_body(h_ref, w_ref, kexp_ref, s1_ref, s2_ref, tau_ref, mlz_ref, se_ref):
    tr = h_ref.shape[0]
    q = jnp.dot(h_ref[...], w_ref[...], preferred_element_type=F32).astype(BF16)
    for c in range(2):
        s = _nt_dot(kexp_ref[c], q[:, c * KH:(c + 1) * KH]) * LOG2E
        if c == 0:
            s1_ref[...] = s
        for lc in range(tr // LANES):
            se_ref[c, lc] = s[:, lc * LANES:(lc + 1) * LANES]

    k = PEER_TOPK
    for lc in range(tr // LANES):
        lanes = slice(lc * LANES, (lc + 1) * LANES)
        for h in range(PEER_HEADS):
            s2_ref[h, :, lanes] = se_ref[1, lc, pl.ds(h, N_KEYS, stride=PEER_HEADS), :]
        a = _top16_of_keys(se_ref.at[0, lc])
        b = _top16_of_keys(se_ref.at[1, lc])
        best = [a[0] + b[r] for r in range(k)]
        lists = [[a[r1] + b[r2] for r2 in range(k // (r1 + 1))] for r1 in range(1, k // 2)]
        lists.append([a[r1] + b[0] for r1 in range(k // 2, k)])
        for lst in lists:
            best = _merge_top(best, lst)
        z = jnp.ones_like(best[0])
        for v in best[1:]:
            z = z + jnp.exp2(v - best[0])
        tau_ref[:, lanes] = best[k - 1]
        mlz_ref[:, lanes] = best[0] + jnp.log2(z)


def _router(h2, w_bf, kexp_bf, tr=256):
    T, D = h2.shape
    qw = w_bf.shape[1]
    return pl.pallas_call(
        _router_body,
        grid=(T // tr,),
        in_specs=[
            pl.BlockSpec((tr, D), lambda i: (i, 0)),
            pl.BlockSpec((D, qw), lambda i: (0, 0)),
            pl.BlockSpec((2, KH, KH), lambda i: (0, 0, 0)),
        ],
        out_specs=[
            pl.BlockSpec((KH, tr), lambda i: (0, i)),
            pl.BlockSpec((PEER_HEADS, N_KEYS, tr), lambda i: (0, 0, i)),
            pl.BlockSpec((PEER_HEADS, tr), lambda i: (0, i)),
            pl.BlockSpec((PEER_HEADS, tr), lambda i: (0, i)),
        ],
        out_shape=[
            jax.ShapeDtypeStruct((KH, T), F32),
            jax.ShapeDtypeStruct((PEER_HEADS, N_KEYS, T), F32),
            jax.ShapeDtypeStruct((PEER_HEADS, T), F32),
            jax.ShapeDtypeStruct((PEER_HEADS, T), F32),
        ],
        scratch_shapes=[pltpu.VMEM((2, tr // LANES, KH, LANES), F32)],
        compiler_params=_params("parallel"),
        name="peer_router",
    )(h2, w_bf, kexp_bf)


def _router_weights(w_pq, keys):
    D = w_pq.shape[0]
    wp = w_pq.reshape(D, PEER_HEADS, 2, N_KEYS).transpose(0, 2, 1, 3).reshape(D, 2 * KH)
    kt = keys.transpose(1, 2, 0, 3)
    same_head = jnp.eye(PEER_HEADS, dtype=bool)[None, None, :, :, None]
    kexp = jnp.where(same_head, kt[:, :, :, None, :], 0.0).reshape(2, KH, KH)
    return wp.astype(BF16), kexp.astype(BF16)


PEER_TC = 128
PEER_SUB = 256


def _peer_body(ht_ref, u_ref, vt_ref, s1_ref, s2_ref, tau_ref, mlz_ref, o_ref, *scratch):
    j = pl.program_id(1)
    te = u_ref.shape[0]
    tt = ht_ref.shape[1]
    nsub = te // PEER_SUB
    a_refs, g_refs = scratch[:nsub], scratch[nsub:]

    @pl.when(j == 0)
    def _():
        o_ref[...] = jnp.zeros_like(o_ref)

    def scores(sub):
        erows = slice(sub * PEER_SUB, (sub + 1) * PEER_SUB)
        a_refs[sub][...] = jnp.dot(u_ref[erows, :], ht_ref[...], preferred_element_type=F32)

    def gate(sub):
        for il in range(PEER_SUB // N_KEYS):
            i1 = sub * (PEER_SUB // N_KEYS) + il
            rows = slice(il * N_KEYS, (il + 1) * N_KEYS)
            for tc in range(tt // PEER_TC):
                lanes = slice(tc * PEER_TC, (tc + 1) * PEER_TC)
                w = None
                for h in range(PEER_HEADS):
                    row = i1 * PEER_HEADS + h
                    z = s2_ref[h, :, lanes] + s1_ref[row:row + 1, lanes]
                    wh = jnp.where(z >= tau_ref[h:h + 1, lanes],
                                   jnp.exp2(z - mlz_ref[h:h + 1, lanes]), 0.0)
                    w = wh if w is None else w + wh
                a = a_refs[sub][rows, lanes]
                gelu = 0.5 * a * (1.0 + lax.erf(a * SQRT_HALF))
                g_refs[sub][rows, lanes] = (w * gelu).astype(BF16)

    def combine(sub):
        erows = slice(sub * PEER_SUB, (sub + 1) * PEER_SUB)
        o_ref[...] += jnp.dot(vt_ref[:, erows], g_refs[sub][...], preferred_element_type=F32)

    scores(0)
    for sub in range(nsub):
        if sub + 1 < nsub:
            scores(sub + 1)
        gate(sub)
        combine(sub)


def _peer_dense(ht, u_bf, vt_bf, s1, s2, tau, mlz, tt=512, te=1024):
    D, T = ht.shape
    E = u_bf.shape[0]
    n_i1 = te // N_KEYS
    nsub = te // PEER_SUB
    return pl.pallas_call(
        _peer_body,
        grid=(T // tt, E // te),
        in_specs=[
            pl.BlockSpec((D, tt), lambda i, j: (0, i)),
            pl.BlockSpec((te, D), lambda i, j: (j, 0)),
            pl.BlockSpec((D, te), lambda i, j: (0, j)),
            pl.BlockSpec((n_i1 * PEER_HEADS, tt), lambda i, j: (j, i)),
            pl.BlockSpec((PEER_HEADS, N_KEYS, tt), lambda i, j: (0, 0, i)),
            pl.BlockSpec((PEER_HEADS, tt), lambda i, j: (0, i)),
            pl.BlockSpec((PEER_HEADS, tt), lambda i, j: (0, i)),
        ],
        out_specs=pl.BlockSpec((D, tt), lambda i, j: (0, i)),
        out_shape=jax.ShapeDtypeStruct((D, T), F32),
        scratch_shapes=([pltpu.VMEM((PEER_SUB, tt), F32)] * nsub + [pltpu.VMEM((PEER_SUB, tt), BF16)] * nsub),
        compiler_params=_params("parallel", "arbitrary"),
        name="peer_dense",
    )(ht, u_bf, vt_bf, s1, s2, tau, mlz)


def _resid_body(x_ref, yt_ref, o_ref):
    o_ref[...] = x_ref[...] + yt_ref[...].T


def _residual_t(xn, yt, tm=512):
    T, D = xn.shape
    return pl.pallas_call(
        _resid_body,
        grid=(T // tm,),
        in_specs=[pl.BlockSpec((tm, D), lambda i: (i, 0)), pl.BlockSpec((D, tm), lambda i: (0, i))],
        out_specs=pl.BlockSpec((tm, D), lambda i: (i, 0)),
        out_shape=jax.ShapeDtypeStruct((T, D), F32),
        compiler_params=_params("parallel"),
        name="peer_residual",
    )(xn, yt)


def _rope_tables(seq):
    inv = ROPE_THETA ** (-jnp.arange(0, ROT_DIM, 2, dtype=jnp.float32) / ROT_DIM)
    ang = jnp.arange(seq, dtype=jnp.float32)[:, None] * inv[None, :]
    cos, sin = jnp.cos(ang), jnp.sin(ang)
    rest = HEAD_DIM - ROT_DIM
    zeros_h = jnp.zeros((seq, ROT_HALF), F32)
    cf = jnp.concatenate([cos, cos, jnp.ones((seq, rest), F32)], axis=-1)
    sa = jnp.concatenate([-sin, zeros_h, jnp.zeros((seq, rest), F32)], axis=-1)
    sb = jnp.concatenate([zeros_h, sin, jnp.zeros((seq, rest), F32)], axis=-1)
    return cf, sa, sb


def kernel(x, mem, attn_norm, w_in, qk_gain, lambda_qk, subln_gain, na_bias, mem_norm, w_mem_kv,
           w_out, ffn_norm, w_pq, peer_keys, peer_u, peer_v):
    B, S, D = x.shape
    depth = w_in.shape[0]
    T = B * S
    cf, sa, sb = _rope_tables(S)
    x2 = x.reshape(T, D)
    for l in range(depth):
        lam_init = 0.8 - 0.6 * math.exp(-0.3 * l)
        proj = _in_proj(x2, attn_norm[l][None], w_in[l].astype(BF16), qk_gain[l], cf, sa, sb, S)
        proj3 = proj.reshape(B, S, IN_WIDTH)
        o_diff = _diff_attention(proj3, lambda_qk[l], subln_gain[l][None], lam_init)
        o_na = _na_attention(proj3, _na_bias_table(na_bias[l], S // GRID_W))
        mk, mv = _mem_kv(mem, mem_norm[l][None], w_mem_kv[l].astype(BF16), qk_gain[l][5:6])
        o_mem = _mem_attention(proj3, mk, mv)
        xn, h2, h2t = _out_proj(x2, o_diff.reshape(T, DIFF_WIDTH), o_na.reshape(T, NA_WIDTH),
                                o_mem.reshape(T, MEM_WIDTH), w_out[l].astype(BF16), ffn_norm[l][None])
        s1, s2, tau, mlz = _router(h2, *_router_weights(w_pq[l], peer_keys[l]))
        yt = _peer_dense(h2t, peer_u[l].astype(BF16), peer_v[l].T.astype(BF16), s1, s2, tau, mlz)
        x2 = _residual_t(xn, yt)
    return x2.reshape(B, S, D)
```

```python
import functools
import math

import jax
import jax.numpy as jnp
from jax import lax
from jax.experimental import pallas as pl
from jax.experimental.pallas import tpu as pltpu

F32 = jnp.float32
BF16 = jnp.bfloat16

HEAD_DIM = 128
N_DIFF_HEADS = 4
N_NA_HEADS = 4
N_MEM_HEADS = 4
DIFF_V_DIM = 2 * HEAD_DIM
DIFF_WIDTH = N_DIFF_HEADS * DIFF_V_DIM
NA_WIDTH = N_NA_HEADS * HEAD_DIM
MEM_WIDTH = N_MEM_HEADS * HEAD_DIM
IN_WIDTH = 3 * DIFF_WIDTH + 3 * NA_WIDTH + MEM_WIDTH
ROPE_THETA = 500000.0
ROT_DIM = HEAD_DIM // 4
ROT_HALF = ROT_DIM // 2
GRID_W = 64
MAX_WIN_H = 8
WIN_W = 16
PEER_HEADS = 8
N_KEYS = 128
PEER_TOPK = 16
EPS = 1e-6
ATTN_SCALE = HEAD_DIM ** -0.5
NEG_BIG = -1e30
SQRT_HALF = math.sqrt(0.5)
LOG2E = math.log2(math.e)

LANES = 128
VMEM_LIMIT = 56 * 1024 * 1024

COL_DQ, COL_DK, COL_DV = 0, 8, 16
COL_NQ, COL_NK, COL_NV, COL_MQ = 24, 28, 32, 36


def _params(*sem):
    return pltpu.CompilerParams(dimension_semantics=sem, vmem_limit_bytes=VMEM_LIMIT)


def _rms(p, g):
    ms = jnp.mean(p * p, axis=-1, keepdims=True)
    return p * lax.rsqrt(ms + EPS) * g


def _nt_dot(a, b):
    return lax.dot_general(a, b, (((1,), (1,)), ((), ())), preferred_element_type=F32)


IN_TN = 1024


def _inproj_body(x_ref, g_ref, w_ref, gq_ref, cf_ref, sa_ref, sb_ref, o_ref, h_ref):
    j = pl.program_id(1)

    @pl.when(j == 0)
    def _():
        h_ref[...] = _rms(x_ref[...], g_ref[...]).astype(BF16)

    acc = jnp.dot(h_ref[...], w_ref[...], preferred_element_type=F32)
    ngrp = IN_TN // HEAD_DIM

    def grp(g):
        return acc[:, g * HEAD_DIM:(g + 1) * HEAD_DIM]

    def put(g, val):
        o_ref[:, g * HEAD_DIM:(g + 1) * HEAD_DIM] = val.astype(BF16)

    def rope(p):
        return (p * cf_ref[...] + pltpu.roll(p, HEAD_DIM - ROT_HALF, 1) * sa_ref[...]
                + pltpu.roll(p, ROT_HALF, 1) * sb_ref[...])

    for jj in (0, 1):
        @pl.when(j == jj)
        def _(jj=jj):
            gain = gq_ref[jj:jj + 1, :]
            for g in range(ngrp):
                put(g, rope(_rms(grp(g), gain)))

    @pl.when(j == 2)
    def _():
        o_ref[...] = acc.astype(BF16)

    @pl.when(j == 3)
    def _():
        for g in range(ngrp):
            row = 2 if g < ngrp // 2 else 3
            put(g, _rms(grp(g), gq_ref[row:row + 1, :]))

    @pl.when(j == 4)
    def _():
        for g in range(ngrp):
            if g < ngrp // 2:
                put(g, grp(g))
            else:
                put(g, _rms(grp(g), gq_ref[4:5, :]))


def _in_proj(x2, g, w_bf, gq, cf, sa, sb, seq, tm=1024):
    T, D = x2.shape
    nseq = seq // tm
    return pl.pallas_call(
        _inproj_body,
        grid=(T // tm, IN_WIDTH // IN_TN),
        in_specs=[
            pl.BlockSpec((tm, D), lambda i, j: (i, 0)),
            pl.BlockSpec((1, D), lambda i, j: (0, 0)),
            pl.BlockSpec((D, IN_TN), lambda i, j: (0, j)),
            pl.BlockSpec((6, HEAD_DIM), lambda i, j: (0, 0)),
            pl.BlockSpec((tm, HEAD_DIM), lambda i, j: (i % nseq, 0)),
            pl.BlockSpec((tm, HEAD_DIM), lambda i, j: (i % nseq, 0)),
            pl.BlockSpec((tm, HEAD_DIM), lambda i, j: (i % nseq, 0)),
        ],
        out_specs=pl.BlockSpec((tm, IN_TN), lambda i, j: (i, j)),
        out_shape=jax.ShapeDtypeStruct((T, IN_WIDTH), BF16),
        scratch_shapes=[pltpu.VMEM((tm, D), BF16)],
        compiler_params=_params("parallel", "arbitrary"),
        name="in_proj",
    )(x2, g, w_bf, gq, cf, sa, sb)


def _diff_body(lq_ref, g_ref, q_ref, k_ref, v_ref, o_ref, *, lam_init):
    lq = lq_ref[...]
    lam = (jnp.exp(jnp.sum(lq[0:1] * lq[1:2], axis=-1, keepdims=True))
           - jnp.exp(jnp.sum(lq[2:3] * lq[3:4], axis=-1, keepdims=True)) + lam_init)

    def attend(c):
        q = q_ref[:, c * HEAD_DIM:(c + 1) * HEAD_DIM]
        k = k_ref[:, c * HEAD_DIM:(c + 1) * HEAD_DIM]
        s = _nt_dot(q, k)
        e = jnp.exp2((s - jnp.max(s, axis=-1, keepdims=True)) * (ATTN_SCALE * LOG2E))
        l = jnp.sum(e, axis=-1, keepdims=True)
        return jnp.dot(e.astype(BF16), v_ref[...], preferred_element_type=F32), l

    o1, l1 = attend(0)
    o2, l2 = attend(1)
    o = o1 * (1.0 / l1) - o2 * (lam / l2)
    o_ref[...] = (_rms(o, g_ref[...]) * (1.0 - lam_init)).astype(o_ref.dtype)


def _diff_attention(proj3, lq, sub_g, lam_init, tq=256):
    B, S, _ = proj3.shape
    return pl.pallas_call(
        functools.partial(_diff_body, lam_init=lam_init),
        grid=(B, N_DIFF_HEADS, S // tq),
        in_specs=[
            pl.BlockSpec((4, HEAD_DIM), lambda b, h, i: (0, 0)),
            pl.BlockSpec((1, DIFF_V_DIM), lambda b, h, i: (0, 0)),
            pl.BlockSpec((None, tq, DIFF_V_DIM), lambda b, h, i: (b, i, COL_DQ // 2 + h)),
            pl.BlockSpec((None, S, DIFF_V_DIM), lambda b, h, i: (b, 0, COL_DK // 2 + h)),
            pl.BlockSpec((None, S, DIFF_V_DIM), lambda b, h, i: (b, 0, COL_DV // 2 + h)),
        ],
        out_specs=pl.BlockSpec((None, tq, DIFF_V_DIM), lambda b, h, i: (b, i, h)),
        out_shape=jax.ShapeDtypeStruct((B, S, DIFF_WIDTH), BF16),
        compiler_params=_params("parallel", "parallel", "arbitrary"),
        name="diff_attn",
    )(lq, sub_g, proj3, proj3, proj3)


NA_QROWS = 8
NA_KROWS = 16
NA_TQ = NA_QROWS * GRID_W
NA_TK = NA_KROWS * GRID_W


def _na_table_body(b_ref, o_ref, *, rows):
    h = pl.program_id(0)
    variant = pl.program_id(1)
    qc = lax.broadcasted_iota(jnp.int32, (GRID_W, GRID_W), 0)
    kc = lax.broadcasted_iota(jnp.int32, (GRID_W, GRID_W), 1)
    cs = jnp.clip(qc - WIN_W // 2, 0, GRID_W - WIN_W)
    col_ok = (kc >= cs) & (kc < cs + WIN_W)
    dcol = kc - qc + (WIN_W - 1)
    masked = jnp.full((GRID_W, GRID_W), NEG_BIG, F32)
    col_tab = []
    for dr in range(2 * MAX_WIN_H - 1):
        t = masked
        for dc in range(2 * WIN_W - 1):
            t = jnp.where(dcol == dc, b_ref[h, dr, dc], t)
        col_tab.append(jnp.where(col_ok, t, NEG_BIG))

    blocks = ((0, 0), (NA_QROWS, NA_QROWS - MAX_WIN_H // 2), (rows - NA_QROWS, rows - NA_KROWS))
    for vi, (r0, kb) in enumerate(blocks):
        @pl.when(variant == vi)
        def _(r0=r0, kb=kb):
            for qr in range(NA_QROWS):
                r = r0 + qr
                rs = min(max(r - MAX_WIN_H // 2, 0), rows - MAX_WIN_H)
                for kp in range(NA_KROWS // 2):
                    halves = []
                    for kr in (2 * kp, 2 * kp + 1):
                        ka = kb + kr
                        inside = rs <= ka < rs + MAX_WIN_H
                        halves.append(col_tab[ka - r + MAX_WIN_H - 1] if inside else masked)
                    o_ref[qr * GRID_W:(qr + 1) * GRID_W, kp * LANES:(kp + 1) * LANES] = (
                        jnp.concatenate(halves, axis=1))


def _na_bias_table(bias, rows):
    nh = bias.shape[0]
    return pl.pallas_call(
        functools.partial(_na_table_body, rows=rows),
        grid=(nh, 3),
        in_specs=[pl.BlockSpec(memory_space=pltpu.SMEM)],
        out_specs=pl.BlockSpec((None, None, NA_TQ, NA_TK), lambda h, v: (h, v, 0, 0)),
        out_shape=jax.ShapeDtypeStruct((nh, 3, NA_TQ, NA_TK), F32),
        compiler_params=_params("parallel", "arbitrary"),
        name="na_bias_table",
    )(bias)


def _na_body(q_ref, k_ref, v_ref, b_ref, o_ref, *, rows):
    i = pl.program_id(2)
    kb = jnp.clip(i * NA_QROWS - MAX_WIN_H // 2, 0, rows - NA_KROWS)
    start = pl.multiple_of(kb * GRID_W, GRID_W)
    k = k_ref[pl.ds(start, NA_TK), :]
    v = v_ref[pl.ds(start, NA_TK), :]
    s = _nt_dot(q_ref[...], k) * ATTN_SCALE + b_ref[...]
    e = jnp.exp(s - jnp.max(s, axis=-1, keepdims=True))
    p = e * (1.0 / jnp.sum(e, axis=-1, keepdims=True))
    o_ref[...] = jnp.dot(p.astype(BF16), v, preferred_element_type=F32).astype(o_ref.dtype)


def _na_attention(proj3, bias_tab):
    B, S, _ = proj3.shape
    rows = S // GRID_W
    nblk = rows // NA_QROWS

    def bias_map(b, h, i):
        return (h, jnp.where(i == 0, 0, jnp.where(i == nblk - 1, 2, 1)), 0, 0)

    return pl.pallas_call(
        functools.partial(_na_body, rows=rows),
        grid=(B, N_NA_HEADS, nblk),
        in_specs=[
            pl.BlockSpec((None, NA_TQ, HEAD_DIM), lambda b, h, i: (b, i, COL_NQ + h)),
            pl.BlockSpec((None, S, HEAD_DIM), lambda b, h, i: (b, 0, COL_NK + h)),
            pl.BlockSpec((None, S, HEAD_DIM), lambda b, h, i: (b, 0, COL_NV + h)),
            pl.BlockSpec((None, None, NA_TQ, NA_TK), bias_map),
        ],
        out_specs=pl.BlockSpec((None, NA_TQ, HEAD_DIM), lambda b, h, i: (b, i, h)),
        out_shape=jax.ShapeDtypeStruct((B, S, NA_WIDTH), BF16),
        compiler_params=_params("parallel", "parallel", "arbitrary"),
        name="na_attn",
    )(proj3, proj3, proj3, bias_tab)


def _memkv_body(m_ref, g_ref, w_ref, gk_ref, k_ref, v_ref):
    hm = _rms(m_ref[...], g_ref[...]).astype(BF16)
    kv = jnp.dot(hm, w_ref[...], preferred_element_type=F32)
    for h in range(N_MEM_HEADS):
        sl = slice(h * HEAD_DIM, (h + 1) * HEAD_DIM)
        k_ref[:, sl] = _rms(kv[:, sl], gk_ref[...]).astype(BF16)
    v_ref[...] = kv[:, MEM_WIDTH:].astype(BF16)


def _mem_kv(mem, g, w_bf, gk):
    B, M, D = mem.shape
    return pl.pallas_call(
        _memkv_body,
        grid=(B,),
        in_specs=[
            pl.BlockSpec((None, M, D), lambda b: (b, 0, 0)),
            pl.BlockSpec((1, D), lambda b: (0, 0)),
            pl.BlockSpec((D, 2 * MEM_WIDTH), lambda b: (0, 0)),
            pl.BlockSpec((1, HEAD_DIM), lambda b: (0, 0)),
        ],
        out_specs=[pl.BlockSpec((None, M, MEM_WIDTH), lambda b: (b, 0, 0))] * 2,
        out_shape=[jax.ShapeDtypeStruct((B, M, MEM_WIDTH), BF16)] * 2,
        compiler_params=_params("parallel"),
        name="mem_kv",
    )(mem, g, w_bf, gk)


def _memattn_body(q_ref, k_ref, v_ref, o_ref):
    for h in range(N_MEM_HEADS):
        sl = slice(h * HEAD_DIM, (h + 1) * HEAD_DIM)
        s = _nt_dot(q_ref[:, sl], k_ref[:, sl]) * ATTN_SCALE
        e = jnp.exp(s - jnp.max(s, axis=-1, keepdims=True))
        p = e * (1.0 / jnp.sum(e, axis=-1, keepdims=True))
        o_ref[:, sl] = jnp.dot(p.astype(BF16), v_ref[:, sl], preferred_element_type=F32).astype(o_ref.dtype)


def _mem_attention(proj3, mk, mv, tq=512):
    B, S, _ = proj3.shape
    M = mk.shape[1]
    return pl.pallas_call(
        _memattn_body,
        grid=(B, S // tq),
        in_specs=[
            pl.BlockSpec((None, tq, MEM_WIDTH), lambda b, i: (b, i, COL_MQ // N_MEM_HEADS)),
            pl.BlockSpec((None, M, MEM_WIDTH), lambda b, i: (b, 0, 0)),
            pl.BlockSpec((None, M, MEM_WIDTH), lambda b, i: (b, 0, 0)),
        ],
        out_specs=pl.BlockSpec((None, tq, MEM_WIDTH), lambda b, i: (b, i, 0)),
        out_shape=jax.ShapeDtypeStruct((B, S, MEM_WIDTH), BF16),
        compiler_params=_params("parallel", "arbitrary"),
        name="mem_attn",
    )(proj3, mk, mv)


def _outproj_body(x_ref, od_ref, on_ref, om_ref, w_ref, g_ref, xn_ref, h_ref, ht_ref):
    mix = jnp.concatenate([od_ref[...], on_ref[...], om_ref[...]], axis=-1)
    xn = x_ref[...] + jnp.dot(mix, w_ref[...], preferred_element_type=F32)
    xn_ref[...] = xn
    h = _rms(xn, g_ref[...])
    h_ref[...] = h.astype(BF16)
    ht_ref[...] = h.T.astype(BF16)


def _out_proj(x2, od, on, om, w_bf, g, tm=256):
    T, D = x2.shape
    return pl.pallas_call(
        _outproj_body,
        grid=(T // tm,),
        in_specs=[
            pl.BlockSpec((tm, D), lambda i: (i, 0)),
            pl.BlockSpec((tm, DIFF_WIDTH), lambda i: (i, 0)),
            pl.BlockSpec((tm, NA_WIDTH), lambda i: (i, 0)),
            pl.BlockSpec((tm, MEM_WIDTH), lambda i: (i, 0)),
            pl.BlockSpec((D, D), lambda i: (0, 0)),
            pl.BlockSpec((1, D), lambda i: (0, 0)),
        ],
        out_specs=[pl.BlockSpec((tm, D), lambda i: (i, 0))] * 2 + [pl.BlockSpec((D, tm), lambda i: (0, i))],
        out_shape=[jax.ShapeDtypeStruct((T, D), F32), jax.ShapeDtypeStruct((T, D), BF16),
                   jax.ShapeDtypeStruct((D, T), BF16)],
        compiler_params=_params("parallel"),
        name="out_proj",
    )(x2, od, on, om, w_bf, g)


KH = N_KEYS * PEER_HEADS


def _sort_pairs(n):
    pairs = []

    def merge(lo, hi, r):
        step = r * 2
        if step < hi - lo:
            merge(lo, hi, step)
            merge(lo + r, hi, step)
            pairs.extend((i, i + r) for i in range(lo + r, hi - r, step))
        else:
            pairs.append((lo, lo + r))

    def sort(lo, hi):
        if hi > lo:
            mid = lo + (hi - lo) // 2
            sort(lo, mid)
            sort(mid + 1, hi)
            merge(lo, hi, 1)

    sort(0, n - 1)
    return pairs


_SORT16 = _sort_pairs(PEER_TOPK)


def _exchange(v, i, j):
    v[i], v[j] = jnp.maximum(v[i], v[j]), jnp.minimum(v[i], v[j])


def _sorted_desc(v):
    v = list(v)
    for i, j in _SORT16:
        _exchange(v, i, j)
    return v


def _merge_top(a, b):
    k = PEER_TOPK
    c = list(a)
    for i in range(len(b)):
        c[k - 1 - i] = jnp.maximum(a[k - 1 - i], b[i])
    d = k // 2
    while d >= 1:
        for i in range(k):
            if i & d == 0:
                _exchange(c, i, i + d)
        d //= 2
    return c


def _top16_of_keys(s_ref):
    best = None
    for g in range(N_KEYS // PEER_TOPK):
        grp = _sorted_desc([s_ref[(g * PEER_TOPK + r) * PEER_HEADS:(g * PEER_TOPK + r + 1) * PEER_HEADS, :]
                            for r in range(PEER_TOPK)])
        best = grp if best is None else _merge_top(best, grp)
    return best


def _router_body(h_ref, w_ref, kexp_ref, s1_ref, s2_ref, tau_ref, mlz_ref, se_ref):
    tr = h_ref.shape[0]
    q = jnp.dot(h_ref[...], w_ref[...], preferred_element_type=F32).astype(BF16)
    for c in range(2):
        s = _nt_dot(kexp_ref[c], q[:, c * KH:(c + 1) * KH]) * LOG2E
        if c == 0:
            s1_ref[...] = s
        for lc in range(tr // LANES):
            se_ref[c, lc] = s[:, lc * LANES:(lc + 1) * LANES]

    k = PEER_TOPK
    for lc in range(tr // LANES):
        lanes = slice(lc * LANES, (lc + 1) * LANES)
        for h in range(PEER_HEADS):
            s2_ref[h, :, lanes] = se_ref[1, lc, pl.ds(h, N_KEYS, stride=PEER_HEADS), :]
        a = _top16_of_keys(se_ref.at[0, lc])
        b = _top16_of_keys(se_ref.at[1, lc])
        best = [a[0] + b[r] for r in range(k)]
        lists = [[a[r1] + b[r2] for r2 in range(k // (r1 + 1))] for r1 in range(1, k // 2)]
        lists.append([a[r1] + b[0] for r1 in range(k // 2, k)])
        for lst in lists:
            best = _merge_top(best, lst)
        z = jnp.ones_like(best[0])
        for v in best[1:]:
            z = z + jnp.exp2(v - best[0])
        tau_ref[:, lanes] = best[k - 1]
        mlz_ref[:, lanes] = best[0] + jnp.log2(z)


def _router(h2, w_bf, kexp_bf, tr=256):
    T, D = h2.shape
    qw = w_bf.shape[1]
    return pl.pallas_call(
        _router_body,
        grid=(T // tr,),
        in_specs=[
            pl.BlockSpec((tr, D), lambda i: (i, 0)),
            pl.BlockSpec((D, qw), lambda i: (0, 0)),
            pl.BlockSpec((2, KH, KH), lambda i: (0, 0, 0)),
        ],
        out_specs=[
            pl.BlockSpec((KH, tr), lambda i: (0, i)),
            pl.BlockSpec((PEER_HEADS, N_KEYS, tr), lambda i: (0, 0, i)),
            pl.BlockSpec((PEER_HEADS, tr), lambda i: (0, i)),
            pl.BlockSpec((PEER_HEADS, tr), lambda i: (0, i)),
        ],
        out_shape=[
            jax.ShapeDtypeStruct((KH, T), F32),
            jax.ShapeDtypeStruct((PEER_HEADS, N_KEYS, T), F32),
            jax.ShapeDtypeStruct((PEER_HEADS, T), F32),
            jax.ShapeDtypeStruct((PEER_HEADS, T), F32),
        ],
        scratch_shapes=[pltpu.VMEM((2, tr // LANES, KH, LANES), F32)],
        compiler_params=_params("parallel"),
        name="peer_router",
    )(h2, w_bf, kexp_bf)


def _router_weights(w_pq, keys):
    D = w_pq.shape[0]
    wp = w_pq.reshape(D, PEER_HEADS, 2, N_KEYS).transpose(0, 2, 1, 3).reshape(D, 2 * KH)
    kt = keys.transpose(1, 2, 0, 3)
    same_head = jnp.eye(PEER_HEADS, dtype=bool)[None, None, :, :, None]
    kexp = jnp.where(same_head, kt[:, :, :, None, :], 0.0).reshape(2, KH, KH)
    return wp.astype(BF16), kexp.astype(BF16)


PEER_TC = 128
PEER_TR = 64
PEER_SUB = 256


def _peer_body(ht_ref, u_ref, vt_ref, s1_ref, s2_ref, tau_ref, mlz_ref, o_ref, g_ref, *a_refs):
    j = pl.program_id(1)
    nsub = len(a_refs)
    tt = ht_ref.shape[1]
    keys_per_sub = PEER_SUB // N_KEYS

    @pl.when(j == 0)
    def _():
        o_ref[...] = jnp.zeros_like(o_ref)

    def scores(k):
        a_refs[k][...] = jnp.dot(u_ref[k * PEER_SUB:(k + 1) * PEER_SUB, :], ht_ref[...],
                                 preferred_element_type=F32)

    def gate(k):
        a_ref = a_refs[k]
        for il in range(keys_per_sub):
            i1 = k * keys_per_sub + il
            for tc in range(tt // PEER_TC):
                lanes = slice(tc * PEER_TC, (tc + 1) * PEER_TC)
                for r0 in range(0, N_KEYS, PEER_TR):
                    keys2 = slice(r0, r0 + PEER_TR)
                    w = None
                    for h in range(PEER_HEADS):
                        z = s2_ref[h, keys2, lanes] + s1_ref[i1, h:h + 1, lanes]
                        wh = jnp.where(z >= tau_ref[h:h + 1, lanes],
                                       jnp.exp2(z - mlz_ref[h:h + 1, lanes]), 0.0)
                        w = wh if w is None else w + wh
                    a = a_ref[il * N_KEYS + r0:il * N_KEYS + r0 + PEER_TR, lanes]
                    gelu = 0.5 * a * (1.0 + lax.erf(a * SQRT_HALF))
                    g_ref[i1 * N_KEYS + r0:i1 * N_KEYS + r0 + PEER_TR, lanes] = (w * gelu).astype(BF16)

    for k in range(nsub):
        scores(k)
        gate(k)
    o_ref[...] += jnp.dot(vt_ref[...], g_ref[...], preferred_element_type=F32)


def _peer_dense(ht, u_bf, vt_bf, s1, s2, tau, mlz, tt=512, te=1024):
    D, T = ht.shape
    E = u_bf.shape[0]
    n_i1 = te // N_KEYS
    nsub = te // PEER_SUB
    return pl.pallas_call(
        _peer_body,
        grid=(T // tt, E // te),
        in_specs=[
            pl.BlockSpec((D, tt), lambda i, j: (0, i)),
            pl.BlockSpec((te, D), lambda i, j: (j, 0)),
            pl.BlockSpec((D, te), lambda i, j: (0, j)),
            pl.BlockSpec((n_i1, PEER_HEADS, tt), lambda i, j: (j, 0, i)),
            pl.BlockSpec((PEER_HEADS, N_KEYS, tt), lambda i, j: (0, 0, i)),
            pl.BlockSpec((PEER_HEADS, tt), lambda i, j: (0, i)),
            pl.BlockSpec((PEER_HEADS, tt), lambda i, j: (0, i)),
        ],
        out_specs=pl.BlockSpec((D, tt), lambda i, j: (0, i)),
        out_shape=jax.ShapeDtypeStruct((D, T), F32),
        scratch_shapes=[pltpu.VMEM((te, tt), BF16)] + [pltpu.VMEM((PEER_SUB, tt), F32)] * nsub,
        compiler_params=_params("parallel", "arbitrary"),
        name="peer_dense",
    )(ht, u_bf, vt_bf, s1.reshape(N_KEYS, PEER_HEADS, T), s2, tau, mlz)


def _resid_body(x_ref, yt_ref, o_ref):
    o_ref[...] = x_ref[...] + yt_ref[...].T


def _residual_t(xn, yt, tm=512):
    T, D = xn.shape
    return pl.pallas_call(
        _resid_body,
        grid=(T // tm,),
        in_specs=[pl.BlockSpec((tm, D), lambda i: (i, 0)), pl.BlockSpec((D, tm), lambda i: (0, i))],
        out_specs=pl.BlockSpec((tm, D), lambda i: (i, 0)),
        out_shape=jax.ShapeDtypeStruct((T, D), F32),
        compiler_params=_params("parallel"),
        name="peer_residual",
    )(xn, yt)


def _rope_tables(seq):
    inv = ROPE_THETA ** (-jnp.arange(0, ROT_DIM, 2, dtype=jnp.float32) / ROT_DIM)
    ang = jnp.arange(seq, dtype=jnp.float32)[:, None] * inv[None, :]
    cos, sin = jnp.cos(ang), jnp.sin(ang)
    rest = HEAD_DIM - ROT_DIM
    zeros_h = jnp.zeros((seq, ROT_HALF), F32)
    cf = jnp.concatenate([cos, cos, jnp.ones((seq, rest), F32)], axis=-1)
    sa = jnp.concatenate([-sin, zeros_h, jnp.zeros((seq, rest), F32)], axis=-1)
    sb = jnp.concatenate([zeros_h, sin, jnp.zeros((seq, rest), F32)], axis=-1)
    return cf, sa, sb


def kernel(x, mem, attn_norm, w_in, qk_gain, lambda_qk, subln_gain, na_bias, mem_norm, w_mem_kv,
           w_out, ffn_norm, w_pq, peer_keys, peer_u, peer_v):
    B, S, D = x.shape
    depth = w_in.shape[0]
    T = B * S
    cf, sa, sb = _rope_tables(S)
    x2 = x.reshape(T, D)
    for l in range(depth):
        lam_init = 0.8 - 0.6 * math.exp(-0.3 * l)
        proj = _in_proj(x2, attn_norm[l][None], w_in[l].astype(BF16), qk_gain[l], cf, sa, sb, S)
        proj3 = proj.reshape(B, S, IN_WIDTH)
        o_diff = _diff_attention(proj3, lambda_qk[l], subln_gain[l][None], lam_init)
        o_na = _na_attention(proj3, _na_bias_table(na_bias[l], S // GRID_W))
        mk, mv = _mem_kv(mem, mem_norm[l][None], w_mem_kv[l].astype(BF16), qk_gain[l][5:6])
        o_mem = _mem_attention(proj3, mk, mv)
        xn, h2, h2t = _out_proj(x2, o_diff.reshape(T, DIFF_WIDTH), o_na.reshape(T, NA_WIDTH),
                                o_mem.reshape(T, MEM_WIDTH), w_out[l].astype(BF16), ffn_norm[l][None])
        s1, s2, tau, mlz = _router(h2, *_router_weights(w_pq[l], peer_keys[l]))
        yt = _peer_dense(h2t, peer_u[l].astype(BF16), peer_v[l].T.astype(BF16), s1, s2, tau, mlz)
        x2 = _residual_t(xn, yt)
    return x2.reshape(B, S, D)
```

```python
import functools
import math

import jax
import jax.numpy as jnp
from jax import lax
from jax.experimental import pallas as pl
from jax.experimental.pallas import tpu as pltpu

F32 = jnp.float32
BF16 = jnp.bfloat16

HEAD_DIM = 128
N_DIFF_HEADS = 4
N_NA_HEADS = 4
N_MEM_HEADS = 4
DIFF_V_DIM = 2 * HEAD_DIM
DIFF_WIDTH = N_DIFF_HEADS * DIFF_V_DIM
NA_WIDTH = N_NA_HEADS * HEAD_DIM
MEM_WIDTH = N_MEM_HEADS * HEAD_DIM
IN_WIDTH = 3 * DIFF_WIDTH + 3 * NA_WIDTH + MEM_WIDTH
ROPE_THETA = 500000.0
ROT_DIM = HEAD_DIM // 4
ROT_HALF = ROT_DIM // 2
GRID_W = 64
MAX_WIN_H = 8
WIN_W = 16
PEER_HEADS = 8
N_KEYS = 128
PEER_TOPK = 16
EPS = 1e-6
ATTN_SCALE = HEAD_DIM ** -0.5
NEG_BIG = -1e30
SQRT_HALF = math.sqrt(0.5)
LOG2E = math.log2(math.e)

LANES = 128
VMEM_LIMIT = 56 * 1024 * 1024

COL_DQ, COL_DK, COL_DV = 0, 8, 16
COL_NQ, COL_NK, COL_NV, COL_MQ = 24, 28, 32, 36


def _params(*sem):
    return pltpu.CompilerParams(dimension_semantics=sem, vmem_limit_bytes=VMEM_LIMIT)


def _rms(p, g):
    ms = jnp.mean(p * p, axis=-1, keepdims=True)
    return p * lax.rsqrt(ms + EPS) * g


def _nt_dot(a, b):
    return lax.dot_general(a, b, (((1,), (1,)), ((), ())), preferred_element_type=F32)


IN_TN = 1024


def _inproj_body(x_ref, g_ref, w_ref, gq_ref, cf_ref, sa_ref, sb_ref, o_ref, h_ref):
    j = pl.program_id(1)

    @pl.when(j == 0)
    def _():
        h_ref[...] = _rms(x_ref[...], g_ref[...]).astype(BF16)

    acc = jnp.dot(h_ref[...], w_ref[...], preferred_element_type=F32)
    ngrp = IN_TN // HEAD_DIM

    def grp(g):
        return acc[:, g * HEAD_DIM:(g + 1) * HEAD_DIM]

    def put(g, val):
        o_ref[:, g * HEAD_DIM:(g + 1) * HEAD_DIM] = val.astype(BF16)

    def rope(p):
        return (p * cf_ref[...] + pltpu.roll(p, HEAD_DIM - ROT_HALF, 1) * sa_ref[...]
                + pltpu.roll(p, ROT_HALF, 1) * sb_ref[...])

    for jj in (0, 1):
        @pl.when(j == jj)
        def _(jj=jj):
            gain = gq_ref[jj:jj + 1, :]
            for g in range(ngrp):
                put(g, rope(_rms(grp(g), gain)))

    @pl.when(j == 2)
    def _():
        o_ref[...] = acc.astype(BF16)

    @pl.when(j == 3)
    def _():
        for g in range(ngrp):
            row = 2 if g < ngrp // 2 else 3
            put(g, _rms(grp(g), gq_ref[row:row + 1, :]))

    @pl.when(j == 4)
    def _():
        for g in range(ngrp):
            if g < ngrp // 2:
                put(g, grp(g))
            else:
                put(g, _rms(grp(g), gq_ref[4:5, :]))


def _in_proj(x2, g, w_bf, gq, cf, sa, sb, seq, tm=1024):
    T, D = x2.shape
    nseq = seq // tm
    return pl.pallas_call(
        _inproj_body,
        grid=(T // tm, IN_WIDTH // IN_TN),
        in_specs=[
            pl.BlockSpec((tm, D), lambda i, j: (i, 0)),
            pl.BlockSpec((1, D), lambda i, j: (0, 0)),
            pl.BlockSpec((D, IN_TN), lambda i, j: (0, j)),
            pl.BlockSpec((6, HEAD_DIM), lambda i, j: (0, 0)),
            pl.BlockSpec((tm, HEAD_DIM), lambda i, j: (i % nseq, 0)),
            pl.BlockSpec((tm, HEAD_DIM), lambda i, j: (i % nseq, 0)),
            pl.BlockSpec((tm, HEAD_DIM), lambda i, j: (i % nseq, 0)),
        ],
        out_specs=pl.BlockSpec((tm, IN_TN), lambda i, j: (i, j)),
        out_shape=jax.ShapeDtypeStruct((T, IN_WIDTH), BF16),
        scratch_shapes=[pltpu.VMEM((tm, D), BF16)],
        compiler_params=_params("parallel", "arbitrary"),
        name="in_proj",
    )(x2, g, w_bf, gq, cf, sa, sb)


def _diff_body(lq_ref, g_ref, q_ref, k_ref, v_ref, o_ref, *, lam_init):
    lq = lq_ref[...]
    lam = (jnp.exp(jnp.sum(lq[0:1] * lq[1:2], axis=-1, keepdims=True))
           - jnp.exp(jnp.sum(lq[2:3] * lq[3:4], axis=-1, keepdims=True)) + lam_init)

    def attend(c):
        q = q_ref[:, c * HEAD_DIM:(c + 1) * HEAD_DIM]
        k = k_ref[:, c * HEAD_DIM:(c + 1) * HEAD_DIM]
        s = _nt_dot(q, k)
        e = jnp.exp2((s - jnp.max(s, axis=-1, keepdims=True)) * (ATTN_SCALE * LOG2E))
        l = jnp.sum(e, axis=-1, keepdims=True)
        return jnp.dot(e.astype(BF16), v_ref[...], preferred_element_type=F32), l

    o1, l1 = attend(0)
    o2, l2 = attend(1)
    o = o1 * (1.0 / l1) - o2 * (lam / l2)
    o_ref[...] = (_rms(o, g_ref[...]) * (1.0 - lam_init)).astype(o_ref.dtype)


def _diff_attention(proj3, lq, sub_g, lam_init, tq=512):
    B, S, _ = proj3.shape
    return pl.pallas_call(
        functools.partial(_diff_body, lam_init=lam_init),
        grid=(B, N_DIFF_HEADS, S // tq),
        in_specs=[
            pl.BlockSpec((4, HEAD_DIM), lambda b, h, i: (0, 0)),
            pl.BlockSpec((1, DIFF_V_DIM), lambda b, h, i: (0, 0)),
            pl.BlockSpec((None, tq, DIFF_V_DIM), lambda b, h, i: (b, i, COL_DQ // 2 + h)),
            pl.BlockSpec((None, S, DIFF_V_DIM), lambda b, h, i: (b, 0, COL_DK // 2 + h)),
            pl.BlockSpec((None, S, DIFF_V_DIM), lambda b, h, i: (b, 0, COL_DV // 2 + h)),
        ],
        out_specs=pl.BlockSpec((None, tq, DIFF_V_DIM), lambda b, h, i: (b, i, h)),
        out_shape=jax.ShapeDtypeStruct((B, S, DIFF_WIDTH), BF16),
        compiler_params=_params("parallel", "parallel", "arbitrary"),
        name="diff_attn",
    )(lq, sub_g, proj3, proj3, proj3)


NA_QROWS = 8
NA_KROWS = 16
NA_TQ = NA_QROWS * GRID_W
NA_TK = NA_KROWS * GRID_W


def _na_table_body(b_ref, o_ref, *, rows):
    h = pl.program_id(0)
    variant = pl.program_id(1)
    qc = lax.broadcasted_iota(jnp.int32, (GRID_W, GRID_W), 0)
    kc = lax.broadcasted_iota(jnp.int32, (GRID_W, GRID_W), 1)
    cs = jnp.clip(qc - WIN_W // 2, 0, GRID_W - WIN_W)
    col_ok = (kc >= cs) & (kc < cs + WIN_W)
    dcol = kc - qc + (WIN_W - 1)
    masked = jnp.full((GRID_W, GRID_W), NEG_BIG, F32)
    col_tab = []
    for dr in range(2 * MAX_WIN_H - 1):
        t = masked
        for dc in range(2 * WIN_W - 1):
            t = jnp.where(dcol == dc, b_ref[h, dr, dc], t)
        col_tab.append(jnp.where(col_ok, t, NEG_BIG))

    blocks = ((0, 0), (NA_QROWS, NA_QROWS - MAX_WIN_H // 2), (rows - NA_QROWS, rows - NA_KROWS))
    for vi, (r0, kb) in enumerate(blocks):
        @pl.when(variant == vi)
        def _(r0=r0, kb=kb):
            for qr in range(NA_QROWS):
                r = r0 + qr
                rs = min(max(r - MAX_WIN_H // 2, 0), rows - MAX_WIN_H)
                for kp in range(NA_KROWS // 2):
                    halves = []
                    for kr in (2 * kp, 2 * kp + 1):
                        ka = kb + kr
                        inside = rs <= ka < rs + MAX_WIN_H
                        halves.append(col_tab[ka - r + MAX_WIN_H - 1] if inside else masked)
                    o_ref[qr * GRID_W:(qr + 1) * GRID_W, kp * LANES:(kp + 1) * LANES] = (
                        jnp.concatenate(halves, axis=1))


def _na_bias_table(bias, rows):
    nh = bias.shape[0]
    return pl.pallas_call(
        functools.partial(_na_table_body, rows=rows),
        grid=(nh, 3),
        in_specs=[pl.BlockSpec(memory_space=pltpu.SMEM)],
        out_specs=pl.BlockSpec((None, None, NA_TQ, NA_TK), lambda h, v: (h, v, 0, 0)),
        out_shape=jax.ShapeDtypeStruct((nh, 3, NA_TQ, NA_TK), F32),
        compiler_params=_params("parallel", "arbitrary"),
        name="na_bias_table",
    )(bias)


def _na_body(q_ref, k_ref, v_ref, b_ref, o_ref, *, rows):
    i = pl.program_id(2)
    kb = jnp.clip(i * NA_QROWS - MAX_WIN_H // 2, 0, rows - NA_KROWS)
    start = pl.multiple_of(kb * GRID_W, GRID_W)
    k = k_ref[pl.ds(start, NA_TK), :]
    v = v_ref[pl.ds(start, NA_TK), :]
    s = _nt_dot(q_ref[...], k) * ATTN_SCALE + b_ref[...]
    e = jnp.exp(s - jnp.max(s, axis=-1, keepdims=True))
    p = e * (1.0 / jnp.sum(e, axis=-1, keepdims=True))
    o_ref[...] = jnp.dot(p.astype(BF16), v, preferred_element_type=F32).astype(o_ref.dtype)


def _na_attention(proj3, bias_tab):
    B, S, _ = proj3.shape
    rows = S // GRID_W
    nblk = rows // NA_QROWS

    def bias_map(b, h, i):
        return (h, jnp.where(i == 0, 0, jnp.where(i == nblk - 1, 2, 1)), 0, 0)

    return pl.pallas_call(
        functools.partial(_na_body, rows=rows),
        grid=(B, N_NA_HEADS, nblk),
        in_specs=[
            pl.BlockSpec((None, NA_TQ, HEAD_DIM), lambda b, h, i: (b, i, COL_NQ + h)),
            pl.BlockSpec((None, S, HEAD_DIM), lambda b, h, i: (b, 0, COL_NK + h)),
            pl.BlockSpec((None, S, HEAD_DIM), lambda b, h, i: (b, 0, COL_NV + h)),
            pl.BlockSpec((None, None, NA_TQ, NA_TK), bias_map),
        ],
        out_specs=pl.BlockSpec((None, NA_TQ, HEAD_DIM), lambda b, h, i: (b, i, h)),
        out_shape=jax.ShapeDtypeStruct((B, S, NA_WIDTH), BF16),
        compiler_params=_params("parallel", "parallel", "arbitrary"),
        name="na_attn",
    )(proj3, proj3, proj3, bias_tab)


def _memkv_body(m_ref, g_ref, w_ref, gk_ref, k_ref, v_ref):
    hm = _rms(m_ref[...], g_ref[...]).astype(BF16)
    kv = jnp.dot(hm, w_ref[...], preferred_element_type=F32)
    for h in range(N_MEM_HEADS):
        sl = slice(h * HEAD_DIM, (h + 1) * HEAD_DIM)
        k_ref[:, sl] = _rms(kv[:, sl], gk_ref[...]).astype(BF16)
    v_ref[...] = kv[:, MEM_WIDTH:].astype(BF16)


def _mem_kv(mem, g, w_bf, gk):
    B, M, D = mem.shape
    return pl.pallas_call(
        _memkv_body,
        grid=(B,),
        in_specs=[
            pl.BlockSpec((None, M, D), lambda b: (b, 0, 0)),
            pl.BlockSpec((1, D), lambda b: (0, 0)),
            pl.BlockSpec((D, 2 * MEM_WIDTH), lambda b: (0, 0)),
            pl.BlockSpec((1, HEAD_DIM), lambda b: (0, 0)),
        ],
        out_specs=[pl.BlockSpec((None, M, MEM_WIDTH), lambda b: (b, 0, 0))] * 2,
        out_shape=[jax.ShapeDtypeStruct((B, M, MEM_WIDTH), BF16)] * 2,
        compiler_params=_params("parallel"),
        name="mem_kv",
    )(mem, g, w_bf, gk)


def _memattn_body(q_ref, k_ref, v_ref, o_ref):
    for h in range(N_MEM_HEADS):
        sl = slice(h * HEAD_DIM, (h + 1) * HEAD_DIM)
        s = _nt_dot(q_ref[:, sl], k_ref[:, sl]) * ATTN_SCALE
        e = jnp.exp(s - jnp.max(s, axis=-1, keepdims=True))
        p = e * (1.0 / jnp.sum(e, axis=-1, keepdims=True))
        o_ref[:, sl] = jnp.dot(p.astype(BF16), v_ref[:, sl], preferred_element_type=F32).astype(o_ref.dtype)


def _mem_attention(proj3, mk, mv, tq=512):
    B, S, _ = proj3.shape
    M = mk.shape[1]
    return pl.pallas_call(
        _memattn_body,
        grid=(B, S // tq),
        in_specs=[
            pl.BlockSpec((None, tq, MEM_WIDTH), lambda b, i: (b, i, COL_MQ // N_MEM_HEADS)),
            pl.BlockSpec((None, M, MEM_WIDTH), lambda b, i: (b, 0, 0)),
            pl.BlockSpec((None, M, MEM_WIDTH), lambda b, i: (b, 0, 0)),
        ],
        out_specs=pl.BlockSpec((None, tq, MEM_WIDTH), lambda b, i: (b, i, 0)),
        out_shape=jax.ShapeDtypeStruct((B, S, MEM_WIDTH), BF16),
        compiler_params=_params("parallel", "arbitrary"),
        name="mem_attn",
    )(proj3, mk, mv)


def _outproj_body(x_ref, od_ref, on_ref, om_ref, w_ref, g_ref, xn_ref, h_ref, ht_ref):
    mix = jnp.concatenate([od_ref[...], on_ref[...], om_ref[...]], axis=-1)
    xn = x_ref[...] + jnp.dot(mix, w_ref[...], preferred_element_type=F32)
    xn_ref[...] = xn
    h = _rms(xn, g_ref[...])
    h_ref[...] = h.astype(BF16)
    ht_ref[...] = h.T.astype(BF16)


def _out_proj(x2, od, on, om, w_bf, g, tm=256):
    T, D = x2.shape
    return pl.pallas_call(
        _outproj_body,
        grid=(T // tm,),
        in_specs=[
            pl.BlockSpec((tm, D), lambda i: (i, 0)),
            pl.BlockSpec((tm, DIFF_WIDTH), lambda i: (i, 0)),
            pl.BlockSpec((tm, NA_WIDTH), lambda i: (i, 0)),
            pl.BlockSpec((tm, MEM_WIDTH), lambda i: (i, 0)),
            pl.BlockSpec((D, D), lambda i: (0, 0)),
            pl.BlockSpec((1, D), lambda i: (0, 0)),
        ],
        out_specs=[pl.BlockSpec((tm, D), lambda i: (i, 0))] * 2 + [pl.BlockSpec((D, tm), lambda i: (0, i))],
        out_shape=[jax.ShapeDtypeStruct((T, D), F32), jax.ShapeDtypeStruct((T, D), BF16),
                   jax.ShapeDtypeStruct((D, T), BF16)],
        compiler_params=_params("parallel"),
        name="out_proj",
    )(x2, od, on, om, w_bf, g)


KH = N_KEYS * PEER_HEADS


def _sort_pairs(n):
    pairs = []

    def merge(lo, hi, r):
        step = r * 2
        if step < hi - lo:
            merge(lo, hi, step)
            merge(lo + r, hi, step)
            pairs.extend((i, i + r) for i in range(lo + r, hi - r, step))
        else:
            pairs.append((lo, lo + r))

    def sort(lo, hi):
        if hi > lo:
            mid = lo + (hi - lo) // 2
            sort(lo, mid)
            sort(mid + 1, hi)
            merge(lo, hi, 1)

    sort(0, n - 1)
    return pairs


_SORT16 = _sort_pairs(PEER_TOPK)


def _exchange(v, i, j):
    v[i], v[j] = jnp.maximum(v[i], v[j]), jnp.minimum(v[i], v[j])


def _sorted_desc(v):
    v = list(v)
    for i, j in _SORT16:
        _exchange(v, i, j)
    return v


def _merge_top(a, b):
    k = PEER_TOPK
    c = list(a)
    for i in range(len(b)):
        c[k - 1 - i] = jnp.maximum(a[k - 1 - i], b[i])
    d = k // 2
    while d >= 1:
        for i in range(k):
            if i & d == 0:
                _exchange(c, i, i + d)
        d //= 2
    return c


def _top16_of_keys(s_ref):
    best = None
    for g in range(N_KEYS // PEER_TOPK):
        grp = _sorted_desc([s_ref[(g * PEER_TOPK + r) * PEER_HEADS:(g * PEER_TOPK + r + 1) * PEER_HEADS, :]
                            for r in range(PEER_TOPK)])
        best = grp if best is None else _merge_top(best, grp)
    return best


def _router_body(h_ref, w_ref, kexp_ref, s1_ref, s2_ref, tau_ref, mlz_ref, se_ref):
    tr = h_ref.shape[0]
    q = jnp.dot(h_ref[...], w_ref[...], preferred_element_type=F32).astype(BF16)
    for c in range(2):
        s = _nt_dot(kexp_ref[c], q[:, c * KH:(c + 1) * KH]) * LOG2E
        if c == 0:
            s1_ref[...] = s
        for lc in range(tr // LANES):
            se_ref[c, lc] = s[:, lc * LANES:(lc + 1) * LANES]

    k = PEER_TOPK
    for lc in range(tr // LANES):
        lanes = slice(lc * LANES, (lc + 1) * LANES)
        for h in range(PEER_HEADS):
            s2_ref[h, :, lanes] = se_ref[1, lc, pl.ds(h, N_KEYS, stride=PEER_HEADS), :]
        a = _top16_of_keys(se_ref.at[0, lc])
        b = _top16_of_keys(se_ref.at[1, lc])
        best = [a[0] + b[r] for r in range(k)]
        lists = [[a[r1] + b[r2] for r2 in range(k // (r1 + 1))] for r1 in range(1, k // 2)]
        lists.append([a[r1] + b[0] for r1 in range(k // 2, k)])
        for lst in lists:
            best = _merge_top(best, lst)
        z = jnp.ones_like(best[0])
        for v in best[1:]:
            z = z + jnp.exp2(v - best[0])
        tau_ref[:, lanes] = best[k - 1]
        mlz_ref[:, lanes] = best[0] + jnp.log2(z)


def _router(h2, w_bf, kexp_bf, tr=256):
    T, D = h2.shape
    qw = w_bf.shape[1]
    return pl.pallas_call(
        _router_body,
        grid=(T // tr,),
        in_specs=[
            pl.BlockSpec((tr, D), lambda i: (i, 0)),
            pl.BlockSpec((D, qw), lambda i: (0, 0)),
            pl.BlockSpec((2, KH, KH), lambda i: (0, 0, 0)),
        ],
        out_specs=[
            pl.BlockSpec((KH, tr), lambda i: (0, i)),
            pl.BlockSpec((PEER_HEADS, N_KEYS, tr), lambda i: (0, 0, i)),
            pl.BlockSpec((PEER_HEADS, tr), lambda i: (0, i)),
            pl.BlockSpec((PEER_HEADS, tr), lambda i: (0, i)),
        ],
        out_shape=[
            jax.ShapeDtypeStruct((KH, T), F32),
            jax.ShapeDtypeStruct((PEER_HEADS, N_KEYS, T), F32),
            jax.ShapeDtypeStruct((PEER_HEADS, T), F32),
            jax.ShapeDtypeStruct((PEER_HEADS, T), F32),
        ],
        scratch_shapes=[pltpu.VMEM((2, tr // LANES, KH, LANES), F32)],
        compiler_params=_params("parallel"),
        name="peer_router",
    )(h2, w_bf, kexp_bf)


def _router_weights(w_pq, keys):
    D = w_pq.shape[0]
    wp = w_pq.reshape(D, PEER_HEADS, 2, N_KEYS).transpose(0, 2, 1, 3).reshape(D, 2 * KH)
    kt = keys.transpose(1, 2, 0, 3)
    same_head = jnp.eye(PEER_HEADS, dtype=bool)[None, None, :, :, None]
    kexp = jnp.where(same_head, kt[:, :, :, None, :], 0.0).reshape(2, KH, KH)
    return wp.astype(BF16), kexp.astype(BF16)


PEER_TC = 128
PEER_TR = 64
PEER_SUB = 256


def _peer_body(ht_ref, u_ref, vt_ref, s1_ref, s2_ref, tau_ref, mlz_ref, o_ref, g_ref, *a_refs):
    j = pl.program_id(1)
    nsub = len(a_refs)
    tt = ht_ref.shape[1]
    keys_per_sub = PEER_SUB // N_KEYS

    @pl.when(j == 0)
    def _():
        o_ref[...] = jnp.zeros_like(o_ref)

    def scores(k):
        a_refs[k][...] = jnp.dot(u_ref[k * PEER_SUB:(k + 1) * PEER_SUB, :], ht_ref[...],
                                 preferred_element_type=F32)

    def gate(k):
        a_ref = a_refs[k]
        for il in range(keys_per_sub):
            i1 = k * keys_per_sub + il
            for tc in range(tt // PEER_TC):
                lanes = slice(tc * PEER_TC, (tc + 1) * PEER_TC)
                for r0 in range(0, N_KEYS, PEER_TR):
                    keys2 = slice(r0, r0 + PEER_TR)
                    w = None
                    for h in range(PEER_HEADS):
                        z = s2_ref[h, keys2, lanes] + s1_ref[i1, h:h + 1, lanes]
                        wh = jnp.where(z >= tau_ref[h:h + 1, lanes],
                                       jnp.exp2(z - mlz_ref[h:h + 1, lanes]), 0.0)
                        w = wh if w is None else w + wh
                    a = a_ref[il * N_KEYS + r0:il * N_KEYS + r0 + PEER_TR, lanes]
                    gelu = 0.5 * a * (1.0 + lax.erf(a * SQRT_HALF))
                    g_ref[i1 * N_KEYS + r0:i1 * N_KEYS + r0 + PEER_TR, lanes] = (w * gelu).astype(BF16)

    for k in range(nsub):
        scores(k)
        gate(k)
    o_ref[...] += jnp.dot(vt_ref[...], g_ref[...], preferred_element_type=F32)


def _peer_dense(ht, u_bf, vt_bf, s1, s2, tau, mlz, tt=1024, te=512):
    D, T = ht.shape
    E = u_bf.shape[0]
    n_i1 = te // N_KEYS
    nsub = te // PEER_SUB
    return pl.pallas_call(
        _peer_body,
        grid=(T // tt, E // te),
        in_specs=[
            pl.BlockSpec((D, tt), lambda i, j: (0, i)),
            pl.BlockSpec((te, D), lambda i, j: (j, 0)),
            pl.BlockSpec((D, te), lambda i, j: (0, j)),
            pl.BlockSpec((n_i1, PEER_HEADS, tt), lambda i, j: (j, 0, i)),
            pl.BlockSpec((PEER_HEADS, N_KEYS, tt), lambda i, j: (0, 0, i)),
            pl.BlockSpec((PEER_HEADS, tt), lambda i, j: (0, i)),
            pl.BlockSpec((PEER_HEADS, tt), lambda i, j: (0, i)),
        ],
        out_specs=pl.BlockSpec((D, tt), lambda i, j: (0, i)),
        out_shape=jax.ShapeDtypeStruct((D, T), F32),
        scratch_shapes=[pltpu.VMEM((te, tt), BF16)] + [pltpu.VMEM((PEER_SUB, tt), F32)] * nsub,
        compiler_params=_params("parallel", "arbitrary"),
        name="peer_dense",
    )(ht, u_bf, vt_bf, s1.reshape(N_KEYS, PEER_HEADS, T), s2, tau, mlz)


def _resid_body(x_ref, yt_ref, o_ref):
    o_ref[...] = x_ref[...] + yt_ref[...].T


def _residual_t(xn, yt, tm=512):
    T, D = xn.shape
    return pl.pallas_call(
        _resid_body,
        grid=(T // tm,),
        in_specs=[pl.BlockSpec((tm, D), lambda i: (i, 0)), pl.BlockSpec((D, tm), lambda i: (0, i))],
        out_specs=pl.BlockSpec((tm, D), lambda i: (i, 0)),
        out_shape=jax.ShapeDtypeStruct((T, D), F32),
        compiler_params=_params("parallel"),
        name="peer_residual",
    )(xn, yt)


def _rope_tables(seq):
    inv = ROPE_THETA ** (-jnp.arange(0, ROT_DIM, 2, dtype=jnp.float32) / ROT_DIM)
    ang = jnp.arange(seq, dtype=jnp.float32)[:, None] * inv[None, :]
    cos, sin = jnp.cos(ang), jnp.sin(ang)
    rest = HEAD_DIM - ROT_DIM
    zeros_h = jnp.zeros((seq, ROT_HALF), F32)
    cf = jnp.concatenate([cos, cos, jnp.ones((seq, rest), F32)], axis=-1)
    sa = jnp.concatenate([-sin, zeros_h, jnp.zeros((seq, rest), F32)], axis=-1)
    sb = jnp.concatenate([zeros_h, sin, jnp.zeros((seq, rest), F32)], axis=-1)
    return cf, sa, sb


def kernel(x, mem, attn_norm, w_in, qk_gain, lambda_qk, subln_gain, na_bias, mem_norm, w_mem_kv,
           w_out, ffn_norm, w_pq, peer_keys, peer_u, peer_v):
    B, S, D = x.shape
    depth = w_in.shape[0]
    T = B * S
    cf, sa, sb = _rope_tables(S)
    x2 = x.reshape(T, D)
    for l in range(depth):
        lam_init = 0.8 - 0.6 * math.exp(-0.3 * l)
        proj = _in_proj(x2, attn_norm[l][None], w_in[l].astype(BF16), qk_gain[l], cf, sa, sb, S)
        proj3 = proj.reshape(B, S, IN_WIDTH)
        o_diff = _diff_attention(proj3, lambda_qk[l], subln_gain[l][None], lam_init)
        o_na = _na_attention(proj3, _na_bias_table(na_bias[l], S // GRID_W))
        mk, mv = _mem_kv(mem, mem_norm[l][None], w_mem_kv[l].astype(BF16), qk_gain[l][5:6])
        o_mem = _mem_attention(proj3, mk, mv)
        xn, h2, h2t = _out_proj(x2, o_diff.reshape(T, DIFF_WIDTH), o_na.reshape(T, NA_WIDTH),
                                o_mem.reshape(T, MEM_WIDTH), w_out[l].astype(BF16), ffn_norm[l][None])
        s1, s2, tau, mlz = _router(h2, *_router_weights(w_pq[l], peer_keys[l]))
        yt = _peer_dense(h2t, peer_u[l].astype(BF16), peer_v[l].T.astype(BF16), s1, s2, tau, mlz)
        x2 = _residual_t(xn, yt)
    return x2.reshape(B, S, D)
```

```python
import functools
import math

import jax
import jax.numpy as jnp
from jax import lax
from jax.experimental import pallas as pl
from jax.experimental.pallas import tpu as pltpu

F32 = jnp.float32
BF16 = jnp.bfloat16

HEAD_DIM = 128
N_DIFF_HEADS = 4
N_NA_HEADS = 4
N_MEM_HEADS = 4
DIFF_V_DIM = 2 * HEAD_DIM
DIFF_WIDTH = N_DIFF_HEADS * DIFF_V_DIM
NA_WIDTH = N_NA_HEADS * HEAD_DIM
MEM_WIDTH = N_MEM_HEADS * HEAD_DIM
IN_WIDTH = 3 * DIFF_WIDTH + 3 * NA_WIDTH + MEM_WIDTH
ROPE_THETA = 500000.0
ROT_DIM = HEAD_DIM // 4
ROT_HALF = ROT_DIM // 2
GRID_W = 64
MAX_WIN_H = 8
WIN_W = 16
PEER_HEADS = 8
N_KEYS = 128
PEER_TOPK = 16
EPS = 1e-6
ATTN_SCALE = HEAD_DIM ** -0.5
NEG_BIG = -1e30
SQRT_HALF = math.sqrt(0.5)
LOG2E = math.log2(math.e)

LANES = 128
VMEM_LIMIT = 56 * 1024 * 1024

COL_DQ, COL_DK, COL_DV = 0, 8, 16
COL_NQ, COL_NK, COL_NV, COL_MQ = 24, 28, 32, 36


def _params(*sem):
    return pltpu.CompilerParams(dimension_semantics=sem, vmem_limit_bytes=VMEM_LIMIT)


def _rms(p, g):
    ms = jnp.mean(p * p, axis=-1, keepdims=True)
    return p * lax.rsqrt(ms + EPS) * g


def _nt_dot(a, b):
    return lax.dot_general(a, b, (((1,), (1,)), ((), ())), preferred_element_type=F32)


IN_TN = 1024


def _inproj_body(x_ref, g_ref, w_ref, gq_ref, cf_ref, sa_ref, sb_ref, o_ref, h_ref):
    j = pl.program_id(1)

    @pl.when(j == 0)
    def _():
        h_ref[...] = _rms(x_ref[...], g_ref[...]).astype(BF16)

    acc = jnp.dot(h_ref[...], w_ref[...], preferred_element_type=F32)
    ngrp = IN_TN // HEAD_DIM

    def grp(g):
        return acc[:, g * HEAD_DIM:(g + 1) * HEAD_DIM]

    def put(g, val):
        o_ref[:, g * HEAD_DIM:(g + 1) * HEAD_DIM] = val.astype(BF16)

    def rope(p):
        return (p * cf_ref[...] + pltpu.roll(p, HEAD_DIM - ROT_HALF, 1) * sa_ref[...]
                + pltpu.roll(p, ROT_HALF, 1) * sb_ref[...])

    for jj in (0, 1):
        @pl.when(j == jj)
        def _(jj=jj):
            gain = gq_ref[jj:jj + 1, :]
            for g in range(ngrp):
                put(g, rope(_rms(grp(g), gain)))

    @pl.when(j == 2)
    def _():
        o_ref[...] = acc.astype(BF16)

    @pl.when(j == 3)
    def _():
        for g in range(ngrp):
            row = 2 if g < ngrp // 2 else 3
            put(g, _rms(grp(g), gq_ref[row:row + 1, :]))

    @pl.when(j == 4)
    def _():
        for g in range(ngrp):
            if g < ngrp // 2:
                put(g, grp(g))
            else:
                put(g, _rms(grp(g), gq_ref[4:5, :]))


def _in_proj(x2, g, w_bf, gq, cf, sa, sb, seq, tm=1024):
    T, D = x2.shape
    nseq = seq // tm
    return pl.pallas_call(
        _inproj_body,
        grid=(T // tm, IN_WIDTH // IN_TN),
        in_specs=[
            pl.BlockSpec((tm, D), lambda i, j: (i, 0)),
            pl.BlockSpec((1, D), lambda i, j: (0, 0)),
            pl.BlockSpec((D, IN_TN), lambda i, j: (0, j)),
            pl.BlockSpec((6, HEAD_DIM), lambda i, j: (0, 0)),
            pl.BlockSpec((tm, HEAD_DIM), lambda i, j: (i % nseq, 0)),
            pl.BlockSpec((tm, HEAD_DIM), lambda i, j: (i % nseq, 0)),
            pl.BlockSpec((tm, HEAD_DIM), lambda i, j: (i % nseq, 0)),
        ],
        out_specs=pl.BlockSpec((tm, IN_TN), lambda i, j: (i, j)),
        out_shape=jax.ShapeDtypeStruct((T, IN_WIDTH), BF16),
        scratch_shapes=[pltpu.VMEM((tm, D), BF16)],
        compiler_params=_params("parallel", "arbitrary"),
        name="in_proj",
    )(x2, g, w_bf, gq, cf, sa, sb)


def _diff_body(lq_ref, g_ref, q_ref, k_ref, v_ref, o_ref, *, lam_init):
    lq = lq_ref[...]
    lam = (jnp.exp(jnp.sum(lq[0:1] * lq[1:2], axis=-1, keepdims=True))
           - jnp.exp(jnp.sum(lq[2:3] * lq[3:4], axis=-1, keepdims=True)) + lam_init)

    def attend(c):
        q = q_ref[:, c * HEAD_DIM:(c + 1) * HEAD_DIM]
        k = k_ref[:, c * HEAD_DIM:(c + 1) * HEAD_DIM]
        s = _nt_dot(q, k)
        e = jnp.exp2((s - jnp.max(s, axis=-1, keepdims=True)) * (ATTN_SCALE * LOG2E))
        l = jnp.sum(e, axis=-1, keepdims=True)
        return jnp.dot(e.astype(BF16), v_ref[...], preferred_element_type=F32), l

    o1, l1 = attend(0)
    o2, l2 = attend(1)
    o = o1 * (1.0 / l1) - o2 * (lam / l2)
    o_ref[...] = (_rms(o, g_ref[...]) * (1.0 - lam_init)).astype(o_ref.dtype)


def _diff_attention(proj3, lq, sub_g, lam_init, tq=512):
    B, S, _ = proj3.shape
    return pl.pallas_call(
        functools.partial(_diff_body, lam_init=lam_init),
        grid=(B, N_DIFF_HEADS, S // tq),
        in_specs=[
            pl.BlockSpec((4, HEAD_DIM), lambda b, h, i: (0, 0)),
            pl.BlockSpec((1, DIFF_V_DIM), lambda b, h, i: (0, 0)),
            pl.BlockSpec((None, tq, DIFF_V_DIM), lambda b, h, i: (b, i, COL_DQ // 2 + h)),
            pl.BlockSpec((None, S, DIFF_V_DIM), lambda b, h, i: (b, 0, COL_DK // 2 + h)),
            pl.BlockSpec((None, S, DIFF_V_DIM), lambda b, h, i: (b, 0, COL_DV // 2 + h)),
        ],
        out_specs=pl.BlockSpec((None, tq, DIFF_V_DIM), lambda b, h, i: (b, i, h)),
        out_shape=jax.ShapeDtypeStruct((B, S, DIFF_WIDTH), BF16),
        compiler_params=_params("parallel", "parallel", "arbitrary"),
        name="diff_attn",
    )(lq, sub_g, proj3, proj3, proj3)


NA_QROWS = 8
NA_KROWS = 16
NA_TQ = NA_QROWS * GRID_W
NA_TK = NA_KROWS * GRID_W


def _na_table_body(b_ref, o_ref, *, rows):
    h = pl.program_id(0)
    variant = pl.program_id(1)
    qc = lax.broadcasted_iota(jnp.int32, (GRID_W, GRID_W), 0)
    kc = lax.broadcasted_iota(jnp.int32, (GRID_W, GRID_W), 1)
    cs = jnp.clip(qc - WIN_W // 2, 0, GRID_W - WIN_W)
    col_ok = (kc >= cs) & (kc < cs + WIN_W)
    dcol = kc - qc + (WIN_W - 1)
    masked = jnp.full((GRID_W, GRID_W), NEG_BIG, F32)
    col_tab = []
    for dr in range(2 * MAX_WIN_H - 1):
        t = masked
        for dc in range(2 * WIN_W - 1):
            t = jnp.where(dcol == dc, b_ref[h, dr, dc], t)
        col_tab.append(jnp.where(col_ok, t, NEG_BIG))

    blocks = ((0, 0), (NA_QROWS, NA_QROWS - MAX_WIN_H // 2), (rows - NA_QROWS, rows - NA_KROWS))
    for vi, (r0, kb) in enumerate(blocks):
        @pl.when(variant == vi)
        def _(r0=r0, kb=kb):
            for qr in range(NA_QROWS):
                r = r0 + qr
                rs = min(max(r - MAX_WIN_H // 2, 0), rows - MAX_WIN_H)
                for kp in range(NA_KROWS // 2):
                    halves = []
                    for kr in (2 * kp, 2 * kp + 1):
                        ka = kb + kr
                        inside = rs <= ka < rs + MAX_WIN_H
                        halves.append(col_tab[ka - r + MAX_WIN_H - 1] if inside else masked)
                    o_ref[qr * GRID_W:(qr + 1) * GRID_W, kp * LANES:(kp + 1) * LANES] = (
                        jnp.concatenate(halves, axis=1))


def _na_bias_table(bias, rows):
    nh = bias.shape[0]
    return pl.pallas_call(
        functools.partial(_na_table_body, rows=rows),
        grid=(nh, 3),
        in_specs=[pl.BlockSpec(memory_space=pltpu.SMEM)],
        out_specs=pl.BlockSpec((None, None, NA_TQ, NA_TK), lambda h, v: (h, v, 0, 0)),
        out_shape=jax.ShapeDtypeStruct((nh, 3, NA_TQ, NA_TK), F32),
        compiler_params=_params("parallel", "arbitrary"),
        name="na_bias_table",
    )(bias)


def _na_body(q_ref, k_ref, v_ref, b_ref, o_ref, *, rows):
    i = pl.program_id(2)
    kb = jnp.clip(i * NA_QROWS - MAX_WIN_H // 2, 0, rows - NA_KROWS)
    start = pl.multiple_of(kb * GRID_W, GRID_W)
    k = k_ref[pl.ds(start, NA_TK), :]
    v = v_ref[pl.ds(start, NA_TK), :]
    s = _nt_dot(q_ref[...], k) * ATTN_SCALE + b_ref[...]
    e = jnp.exp(s - jnp.max(s, axis=-1, keepdims=True))
    p = e * (1.0 / jnp.sum(e, axis=-1, keepdims=True))
    o_ref[...] = jnp.dot(p.astype(BF16), v, preferred_element_type=F32).astype(o_ref.dtype)


def _na_attention(proj3, bias_tab):
    B, S, _ = proj3.shape
    rows = S // GRID_W
    nblk = rows // NA_QROWS

    def bias_map(b, h, i):
        return (h, jnp.where(i == 0, 0, jnp.where(i == nblk - 1, 2, 1)), 0, 0)

    return pl.pallas_call(
        functools.partial(_na_body, rows=rows),
        grid=(B, N_NA_HEADS, nblk),
        in_specs=[
            pl.BlockSpec((None, NA_TQ, HEAD_DIM), lambda b, h, i: (b, i, COL_NQ + h)),
            pl.BlockSpec((None, S, HEAD_DIM), lambda b, h, i: (b, 0, COL_NK + h)),
            pl.BlockSpec((None, S, HEAD_DIM), lambda b, h, i: (b, 0, COL_NV + h)),
            pl.BlockSpec((None, None, NA_TQ, NA_TK), bias_map),
        ],
        out_specs=pl.BlockSpec((None, NA_TQ, HEAD_DIM), lambda b, h, i: (b, i, h)),
        out_shape=jax.ShapeDtypeStruct((B, S, NA_WIDTH), BF16),
        compiler_params=_params("parallel", "parallel", "arbitrary"),
        name="na_attn",
    )(proj3, proj3, proj3, bias_tab)


def _memkv_body(m_ref, g_ref, w_ref, gk_ref, k_ref, v_ref):
    hm = _rms(m_ref[...], g_ref[...]).astype(BF16)
    kv = jnp.dot(hm, w_ref[...], preferred_element_type=F32)
    for h in range(N_MEM_HEADS):
        sl = slice(h * HEAD_DIM, (h + 1) * HEAD_DIM)
        k_ref[:, sl] = _rms(kv[:, sl], gk_ref[...]).astype(BF16)
    v_ref[...] = kv[:, MEM_WIDTH:].astype(BF16)


def _mem_kv(mem, g, w_bf, gk):
    B, M, D = mem.shape
    return pl.pallas_call(
        _memkv_body,
        grid=(B,),
        in_specs=[
            pl.BlockSpec((None, M, D), lambda b: (b, 0, 0)),
            pl.BlockSpec((1, D), lambda b: (0, 0)),
            pl.BlockSpec((D, 2 * MEM_WIDTH), lambda b: (0, 0)),
            pl.BlockSpec((1, HEAD_DIM), lambda b: (0, 0)),
        ],
        out_specs=[pl.BlockSpec((None, M, MEM_WIDTH), lambda b: (b, 0, 0))] * 2,
        out_shape=[jax.ShapeDtypeStruct((B, M, MEM_WIDTH), BF16)] * 2,
        compiler_params=_params("parallel"),
        name="mem_kv",
    )(mem, g, w_bf, gk)


def _memattn_body(q_ref, k_ref, v_ref, o_ref):
    for h in range(N_MEM_HEADS):
        sl = slice(h * HEAD_DIM, (h + 1) * HEAD_DIM)
        s = _nt_dot(q_ref[:, sl], k_ref[:, sl]) * ATTN_SCALE
        e = jnp.exp(s - jnp.max(s, axis=-1, keepdims=True))
        p = e * (1.0 / jnp.sum(e, axis=-1, keepdims=True))
        o_ref[:, sl] = jnp.dot(p.astype(BF16), v_ref[:, sl], preferred_element_type=F32).astype(o_ref.dtype)


def _mem_attention(proj3, mk, mv, tq=512):
    B, S, _ = proj3.shape
    M = mk.shape[1]
    return pl.pallas_call(
        _memattn_body,
        grid=(B, S // tq),
        in_specs=[
            pl.BlockSpec((None, tq, MEM_WIDTH), lambda b, i: (b, i, COL_MQ // N_MEM_HEADS)),
            pl.BlockSpec((None, M, MEM_WIDTH), lambda b, i: (b, 0, 0)),
            pl.BlockSpec((None, M, MEM_WIDTH), lambda b, i: (b, 0, 0)),
        ],
        out_specs=pl.BlockSpec((None, tq, MEM_WIDTH), lambda b, i: (b, i, 0)),
        out_shape=jax.ShapeDtypeStruct((B, S, MEM_WIDTH), BF16),
        compiler_params=_params("parallel", "arbitrary"),
        name="mem_attn",
    )(proj3, mk, mv)


def _outproj_body(x_ref, od_ref, on_ref, om_ref, w_ref, g_ref, xn_ref, h_ref, ht_ref):
    mix = jnp.concatenate([od_ref[...], on_ref[...], om_ref[...]], axis=-1)
    xn = x_ref[...] + jnp.dot(mix, w_ref[...], preferred_element_type=F32)
    xn_ref[...] = xn
    h = _rms(xn, g_ref[...])
    h_ref[...] = h.astype(BF16)
    ht_ref[...] = h.T.astype(BF16)


def _out_proj(x2, od, on, om, w_bf, g, tm=256):
    T, D = x2.shape
    return pl.pallas_call(
        _outproj_body,
        grid=(T // tm,),
        in_specs=[
            pl.BlockSpec((tm, D), lambda i: (i, 0)),
            pl.BlockSpec((tm, DIFF_WIDTH), lambda i: (i, 0)),
            pl.BlockSpec((tm, NA_WIDTH), lambda i: (i, 0)),
            pl.BlockSpec((tm, MEM_WIDTH), lambda i: (i, 0)),
            pl.BlockSpec((D, D), lambda i: (0, 0)),
            pl.BlockSpec((1, D), lambda i: (0, 0)),
        ],
        out_specs=[pl.BlockSpec((tm, D), lambda i: (i, 0))] * 2 + [pl.BlockSpec((D, tm), lambda i: (0, i))],
        out_shape=[jax.ShapeDtypeStruct((T, D), F32), jax.ShapeDtypeStruct((T, D), BF16),
                   jax.ShapeDtypeStruct((D, T), BF16)],
        compiler_params=_params("parallel"),
        name="out_proj",
    )(x2, od, on, om, w_bf, g)


KH = N_KEYS * PEER_HEADS


def _sort_pairs(n):
    pairs = []

    def merge(lo, hi, r):
        step = r * 2
        if step < hi - lo:
            merge(lo, hi, step)
            merge(lo + r, hi, step)
            pairs.extend((i, i + r) for i in range(lo + r, hi - r, step))
        else:
            pairs.append((lo, lo + r))

    def sort(lo, hi):
        if hi > lo:
            mid = lo + (hi - lo) // 2
            sort(lo, mid)
            sort(mid + 1, hi)
            merge(lo, hi, 1)

    sort(0, n - 1)
    return pairs


_SORT16 = _sort_pairs(PEER_TOPK)


def _exchange(v, i, j):
    v[i], v[j] = jnp.maximum(v[i], v[j]), jnp.minimum(v[i], v[j])


def _sorted_desc(v):
    v = list(v)
    for i, j in _SORT16:
        _exchange(v, i, j)
    return v


def _merge_top(a, b):
    k = PEER_TOPK
    c = list(a)
    for i in range(len(b)):
        c[k - 1 - i] = jnp.maximum(a[k - 1 - i], b[i])
    d = k // 2
    while d >= 1:
        for i in range(k):
            if i & d == 0:
                _exchange(c, i, i + d)
        d //= 2
    return c


def _top16_of_keys(s_ref):
    best = None
    for g in range(N_KEYS // PEER_TOPK):
        grp = _sorted_desc([s_ref[(g * PEER_TOPK + r) * PEER_HEADS:(g * PEER_TOPK + r + 1) * PEER_HEADS, :]
                            for r in range(PEER_TOPK)])
        best = grp if best is None else _merge_top(best, grp)
    return best


def _router_body(h_ref, w_ref, kexp_ref, s1_ref, s2_ref, b_ref, tau_ref, mlz_ref, se_ref):
    tr = h_ref.shape[0]
    q = jnp.dot(h_ref[...], w_ref[...], preferred_element_type=F32).astype(BF16)
    for c in range(2):
        s = _nt_dot(kexp_ref[c], q[:, c * KH:(c + 1) * KH]) * LOG2E
        if c == 0:
            s1_ref[...] = s
        for lc in range(tr // LANES):
            se_ref[c, lc] = s[:, lc * LANES:(lc + 1) * LANES]

    k = PEER_TOPK
    for lc in range(tr // LANES):
        lanes = slice(lc * LANES, (lc + 1) * LANES)
        for h in range(PEER_HEADS):
            s2_ref[h, :, lanes] = se_ref[1, lc, pl.ds(h, N_KEYS, stride=PEER_HEADS), :]
        a = _top16_of_keys(se_ref.at[0, lc])
        b = _top16_of_keys(se_ref.at[1, lc])
        best = [a[0] + b[r] for r in range(k)]
        lists = [[a[r1] + b[r2] for r2 in range(k // (r1 + 1))] for r1 in range(1, k // 2)]
        lists.append([a[r1] + b[0] for r1 in range(k // 2, k)])
        for lst in lists:
            best = _merge_top(best, lst)
        z = jnp.ones_like(best[0])
        for v in best[1:]:
            z = z + jnp.exp2(v - best[0])
        for r in range(k):
            b_ref[r, :, lanes] = b[r]
        tau_ref[:, lanes] = best[k - 1]
        mlz_ref[:, lanes] = best[0] + jnp.log2(z) + 1.0


def _router(h2, w_bf, kexp_bf, tr=256):
    T, D = h2.shape
    qw = w_bf.shape[1]
    return pl.pallas_call(
        _router_body,
        grid=(T // tr,),
        in_specs=[
            pl.BlockSpec((tr, D), lambda i: (i, 0)),
            pl.BlockSpec((D, qw), lambda i: (0, 0)),
            pl.BlockSpec((2, KH, KH), lambda i: (0, 0, 0)),
        ],
        out_specs=[
            pl.BlockSpec((KH, tr), lambda i: (0, i)),
            pl.BlockSpec((PEER_HEADS, N_KEYS, tr), lambda i: (0, 0, i)),
            pl.BlockSpec((PEER_TOPK, PEER_HEADS, tr), lambda i: (0, 0, i)),
            pl.BlockSpec((PEER_HEADS, tr), lambda i: (0, i)),
            pl.BlockSpec((PEER_HEADS, tr), lambda i: (0, i)),
        ],
        out_shape=[
            jax.ShapeDtypeStruct((KH, T), F32),
            jax.ShapeDtypeStruct((PEER_HEADS, N_KEYS, T), F32),
            jax.ShapeDtypeStruct((PEER_TOPK, PEER_HEADS, T), F32),
            jax.ShapeDtypeStruct((PEER_HEADS, T), F32),
            jax.ShapeDtypeStruct((PEER_HEADS, T), F32),
        ],
        scratch_shapes=[pltpu.VMEM((2, tr // LANES, KH, LANES), F32)],
        compiler_params=_params("parallel"),
        name="peer_router",
    )(h2, w_bf, kexp_bf)


def _router_weights(w_pq, keys):
    D = w_pq.shape[0]
    wp = w_pq.reshape(D, PEER_HEADS, 2, N_KEYS).transpose(0, 2, 1, 3).reshape(D, 2 * KH)
    kt = keys.transpose(1, 2, 0, 3)
    same_head = jnp.eye(PEER_HEADS, dtype=bool)[None, None, :, :, None]
    kexp = jnp.where(same_head, kt[:, :, :, None, :], 0.0).reshape(2, KH, KH)
    return wp.astype(BF16), kexp.astype(BF16)


PEER_TC = 128
PEER_TR = 64
PEER_SUB = 256


def _peer_body(ht_ref, u_ref, vt_ref, s1_ref, s2_ref, b_ref, tau_ref, mlz_ref, o_ref,
               thr_ref, off_ref, g_ref, *a_refs):
    j = pl.program_id(1)
    nsub = len(a_refs)
    tt = ht_ref.shape[1]
    n_i1 = s1_ref.shape[0]
    keys_per_sub = PEER_SUB // N_KEYS
    sub8 = PEER_TR // 8

    @pl.when(j == 0)
    def _():
        o_ref[...] = jnp.zeros_like(o_ref)

    for il in range(n_i1):
        for tc in range(tt // PEER_TC):
            lanes = slice(tc * PEER_TC, (tc + 1) * PEER_TC)
            s1 = s1_ref[il, :, lanes]
            tau = tau_ref[:, lanes]
            thr = jnp.full_like(s1, jnp.inf)
            for r in range(PEER_TOPK):
                b = b_ref[r, :, lanes]
                thr = jnp.where(s1 + b >= tau, b, thr)
            off = s1 - mlz_ref[:, lanes]
            for h in range(PEER_HEADS):
                row = il * PEER_HEADS + h
                thr_ref[row, :, lanes] = jnp.broadcast_to(thr[h:h + 1, :], (8, PEER_TC))
                off_ref[row, :, lanes] = jnp.broadcast_to(off[h:h + 1, :], (8, PEER_TC))

    def scores(k):
        a_refs[k][...] = jnp.dot(u_ref[k * PEER_SUB:(k + 1) * PEER_SUB, :], ht_ref[...],
                                 preferred_element_type=F32)

    def gate(k):
        a_ref = a_refs[k]
        for il in range(keys_per_sub):
            i1 = k * keys_per_sub + il
            for tc in range(tt // PEER_TC):
                lanes = slice(tc * PEER_TC, (tc + 1) * PEER_TC)
                for r0 in range(0, N_KEYS, PEER_TR):
                    w = None
                    for h in range(PEER_HEADS):
                        row = i1 * PEER_HEADS + h
                        s2 = s2_ref[h, r0:r0 + PEER_TR, lanes].reshape(sub8, 8, PEER_TC)
                        wh = jnp.where(s2 >= thr_ref[row, :, lanes][None],
                                       jnp.exp2(s2 + off_ref[row, :, lanes][None]), 0.0)
                        w = wh if w is None else w + wh
                    a = a_ref[il * N_KEYS + r0:il * N_KEYS + r0 + PEER_TR, lanes].reshape(sub8, 8, PEER_TC)
                    g = w * (a * (1.0 + lax.erf(a * SQRT_HALF)))
                    g_ref[i1 * N_KEYS + r0:i1 * N_KEYS + r0 + PEER_TR, lanes] = (
                        g.reshape(PEER_TR, PEER_TC).astype(BF16))

    for k in range(nsub):
        scores(k)
        gate(k)
    o_ref[...] += jnp.dot(vt_ref[...], g_ref[...], preferred_element_type=F32)


def _peer_dense(ht, u_bf, vt_bf, s1, s2, b2, tau, mlz, tt=512, te=1024):
    D, T = ht.shape
    E = u_bf.shape[0]
    n_i1 = te // N_KEYS
    nsub = te // PEER_SUB
    return pl.pallas_call(
        _peer_body,
        grid=(T // tt, E // te),
        in_specs=[
            pl.BlockSpec((D, tt), lambda i, j: (0, i)),
            pl.BlockSpec((te, D), lambda i, j: (j, 0)),
            pl.BlockSpec((D, te), lambda i, j: (0, j)),
            pl.BlockSpec((n_i1, PEER_HEADS, tt), lambda i, j: (j, 0, i)),
            pl.BlockSpec((PEER_HEADS, N_KEYS, tt), lambda i, j: (0, 0, i)),
            pl.BlockSpec((PEER_TOPK, PEER_HEADS, tt), lambda i, j: (0, 0, i)),
            pl.BlockSpec((PEER_HEADS, tt), lambda i, j: (0, i)),
            pl.BlockSpec((PEER_HEADS, tt), lambda i, j: (0, i)),
        ],
        out_specs=pl.BlockSpec((D, tt), lambda i, j: (0, i)),
        out_shape=jax.ShapeDtypeStruct((D, T), F32),
        scratch_shapes=([pltpu.VMEM((n_i1 * PEER_HEADS, 8, tt), F32)] * 2 + [pltpu.VMEM((te, tt), BF16)]
                        + [pltpu.VMEM((PEER_SUB, tt), F32)] * nsub),
        compiler_params=_params("parallel", "arbitrary"),
        name="peer_dense",
    )(ht, u_bf, vt_bf, s1.reshape(N_KEYS, PEER_HEADS, T), s2, b2, tau, mlz)


def _resid_body(x_ref, yt_ref, o_ref):
    o_ref[...] = x_ref[...] + yt_ref[...].T


def _residual_t(xn, yt, tm=512):
    T, D = xn.shape
    return pl.pallas_call(
        _resid_body,
        grid=(T // tm,),
        in_specs=[pl.BlockSpec((tm, D), lambda i: (i, 0)), pl.BlockSpec((D, tm), lambda i: (0, i))],
        out_specs=pl.BlockSpec((tm, D), lambda i: (i, 0)),
        out_shape=jax.ShapeDtypeStruct((T, D), F32),
        compiler_params=_params("parallel"),
        name="peer_residual",
    )(xn, yt)


def _rope_tables(seq):
    inv = ROPE_THETA ** (-jnp.arange(0, ROT_DIM, 2, dtype=jnp.float32) / ROT_DIM)
    ang = jnp.arange(seq, dtype=jnp.float32)[:, None] * inv[None, :]
    cos, sin = jnp.cos(ang), jnp.sin(ang)
    rest = HEAD_DIM - ROT_DIM
    zeros_h = jnp.zeros((seq, ROT_HALF), F32)
    cf = jnp.concatenate([cos, cos, jnp.ones((seq, rest), F32)], axis=-1)
    sa = jnp.concatenate([-sin, zeros_h, jnp.zeros((seq, rest), F32)], axis=-1)
    sb = jnp.concatenate([zeros_h, sin, jnp.zeros((seq, rest), F32)], axis=-1)
    return cf, sa, sb


def kernel(x, mem, attn_norm, w_in, qk_gain, lambda_qk, subln_gain, na_bias, mem_norm, w_mem_kv,
           w_out, ffn_norm, w_pq, peer_keys, peer_u, peer_v):
    B, S, D = x.shape
    depth = w_in.shape[0]
    T = B * S
    cf, sa, sb = _rope_tables(S)
    x2 = x.reshape(T, D)
    for l in range(depth):
        lam_init = 0.8 - 0.6 * math.exp(-0.3 * l)
        proj = _in_proj(x2, attn_norm[l][None], w_in[l].astype(BF16), qk_gain[l], cf, sa, sb, S)
        proj3 = proj.reshape(B, S, IN_WIDTH)
        o_diff = _diff_attention(proj3, lambda_qk[l], subln_gain[l][None], lam_init)
        o_na = _na_attention(proj3, _na_bias_table(na_bias[l], S // GRID_W))
        mk, mv = _mem_kv(mem, mem_norm[l][None], w_mem_kv[l].astype(BF16), qk_gain[l][5:6])
        o_mem = _mem_attention(proj3, mk, mv)
        xn, h2, h2t = _out_proj(x2, o_diff.reshape(T, DIFF_WIDTH), o_na.reshape(T, NA_WIDTH),
                                o_mem.reshape(T, MEM_WIDTH), w_out[l].astype(BF16), ffn_norm[l][None])
        s1, s2, b2, tau, mlz = _router(h2, *_router_weights(w_pq[l], peer_keys[l]))
        yt = _peer_dense(h2t, peer_u[l].astype(BF16), peer_v[l].T.astype(BF16), s1, s2, b2, tau, mlz)
        x2 = _residual_t(xn, yt)
    return x2.reshape(B, S, D)
```

```python
import functools
import math

import jax
import jax.numpy as jnp
from jax import lax
from jax.experimental import pallas as pl
from jax.experimental.pallas import tpu as pltpu

F32 = jnp.float32
BF16 = jnp.bfloat16

HEAD_DIM = 128
N_DIFF_HEADS = 4
N_NA_HEADS = 4
N_MEM_HEADS = 4
DIFF_V_DIM = 2 * HEAD_DIM
DIFF_WIDTH = N_DIFF_HEADS * DIFF_V_DIM
NA_WIDTH = N_NA_HEADS * HEAD_DIM
MEM_WIDTH = N_MEM_HEADS * HEAD_DIM
IN_WIDTH = 3 * DIFF_WIDTH + 3 * NA_WIDTH + MEM_WIDTH
ROPE_THETA = 500000.0
ROT_DIM = HEAD_DIM // 4
ROT_HALF = ROT_DIM // 2
GRID_W = 64
MAX_WIN_H = 8
WIN_W = 16
PEER_HEADS = 8
N_KEYS = 128
PEER_TOPK = 16
EPS = 1e-6
ATTN_SCALE = HEAD_DIM ** -0.5
NEG_BIG = -1e30
SQRT_HALF = math.sqrt(0.5)
LOG2E = math.log2(math.e)

LANES = 128
VMEM_LIMIT = 56 * 1024 * 1024

COL_DQ, COL_DK, COL_DV = 0, 8, 16
COL_NQ, COL_NK, COL_NV, COL_MQ = 24, 28, 32, 36


def _params(*sem):
    return pltpu.CompilerParams(dimension_semantics=sem, vmem_limit_bytes=VMEM_LIMIT)


def _rms(p, g):
    ms = jnp.mean(p * p, axis=-1, keepdims=True)
    return p * lax.rsqrt(ms + EPS) * g


def _nt_dot(a, b):
    return lax.dot_general(a, b, (((1,), (1,)), ((), ())), preferred_element_type=F32)


IN_TN = 1024


def _inproj_body(x_ref, g_ref, w_ref, gq_ref, cf_ref, sa_ref, sb_ref, o_ref, h_ref):
    j = pl.program_id(1)

    @pl.when(j == 0)
    def _():
        h_ref[...] = _rms(x_ref[...], g_ref[...]).astype(BF16)

    acc = jnp.dot(h_ref[...], w_ref[...], preferred_element_type=F32)
    ngrp = IN_TN // HEAD_DIM

    def grp(g):
        return acc[:, g * HEAD_DIM:(g + 1) * HEAD_DIM]

    def put(g, val):
        o_ref[:, g * HEAD_DIM:(g + 1) * HEAD_DIM] = val.astype(BF16)

    def rope(p):
        return (p * cf_ref[...] + pltpu.roll(p, HEAD_DIM - ROT_HALF, 1) * sa_ref[...]
                + pltpu.roll(p, ROT_HALF, 1) * sb_ref[...])

    for jj in (0, 1):
        @pl.when(j == jj)
        def _(jj=jj):
            gain = gq_ref[jj:jj + 1, :]
            for g in range(ngrp):
                put(g, rope(_rms(grp(g), gain)))

    @pl.when(j == 2)
    def _():
        o_ref[...] = acc.astype(BF16)

    @pl.when(j == 3)
    def _():
        for g in range(ngrp):
            row = 2 if g < ngrp // 2 else 3
            put(g, _rms(grp(g), gq_ref[row:row + 1, :]))

    @pl.when(j == 4)
    def _():
        for g in range(ngrp):
            if g < ngrp // 2:
                put(g, grp(g))
            else:
                put(g, _rms(grp(g), gq_ref[4:5, :]))


def _in_proj(x2, g, w_bf, gq, cf, sa, sb, seq, tm=1024):
    T, D = x2.shape
    nseq = seq // tm
    return pl.pallas_call(
        _inproj_body,
        grid=(T // tm, IN_WIDTH // IN_TN),
        in_specs=[
            pl.BlockSpec((tm, D), lambda i, j: (i, 0)),
            pl.BlockSpec((1, D), lambda i, j: (0, 0)),
            pl.BlockSpec((D, IN_TN), lambda i, j: (0, j)),
            pl.BlockSpec((6, HEAD_DIM), lambda i, j: (0, 0)),
            pl.BlockSpec((tm, HEAD_DIM), lambda i, j: (i % nseq, 0)),
            pl.BlockSpec((tm, HEAD_DIM), lambda i, j: (i % nseq, 0)),
            pl.BlockSpec((tm, HEAD_DIM), lambda i, j: (i % nseq, 0)),
        ],
        out_specs=pl.BlockSpec((tm, IN_TN), lambda i, j: (i, j)),
        out_shape=jax.ShapeDtypeStruct((T, IN_WIDTH), BF16),
        scratch_shapes=[pltpu.VMEM((tm, D), BF16)],
        compiler_params=_params("parallel", "arbitrary"),
        name="in_proj",
    )(x2, g, w_bf, gq, cf, sa, sb)


DIFF_SUBQ = 256


def _diff_body(lq_ref, g_ref, q_ref, k_ref, v_ref, o_ref, *, lam_init):
    lq = lq_ref[...]
    lam = (jnp.exp(jnp.sum(lq[0:1] * lq[1:2], axis=-1, keepdims=True))
           - jnp.exp(jnp.sum(lq[2:3] * lq[3:4], axis=-1, keepdims=True)) + lam_init)

    def attend(rows, c):
        q = q_ref[rows, c * HEAD_DIM:(c + 1) * HEAD_DIM]
        k = k_ref[:, c * HEAD_DIM:(c + 1) * HEAD_DIM]
        s = _nt_dot(q, k)
        e = jnp.exp2((s - jnp.max(s, axis=-1, keepdims=True)) * (ATTN_SCALE * LOG2E))
        l = jnp.sum(e, axis=-1, keepdims=True)
        return jnp.dot(e.astype(BF16), v_ref[...], preferred_element_type=F32), l

    for r0 in range(0, q_ref.shape[0], DIFF_SUBQ):
        rows = slice(r0, r0 + DIFF_SUBQ)
        o1, l1 = attend(rows, 0)
        o2, l2 = attend(rows, 1)
        o = o1 * (1.0 / l1) - o2 * (lam / l2)
        o_ref[rows, :] = (_rms(o, g_ref[...]) * (1.0 - lam_init)).astype(o_ref.dtype)


def _diff_attention(proj3, lq, sub_g, lam_init, tq=512):
    B, S, _ = proj3.shape
    return pl.pallas_call(
        functools.partial(_diff_body, lam_init=lam_init),
        grid=(B, N_DIFF_HEADS, S // tq),
        in_specs=[
            pl.BlockSpec((4, HEAD_DIM), lambda b, h, i: (0, 0)),
            pl.BlockSpec((1, DIFF_V_DIM), lambda b, h, i: (0, 0)),
            pl.BlockSpec((None, tq, DIFF_V_DIM), lambda b, h, i: (b, i, COL_DQ // 2 + h)),
            pl.BlockSpec((None, S, DIFF_V_DIM), lambda b, h, i: (b, 0, COL_DK // 2 + h)),
            pl.BlockSpec((None, S, DIFF_V_DIM), lambda b, h, i: (b, 0, COL_DV // 2 + h)),
        ],
        out_specs=pl.BlockSpec((None, tq, DIFF_V_DIM), lambda b, h, i: (b, i, h)),
        out_shape=jax.ShapeDtypeStruct((B, S, DIFF_WIDTH), BF16),
        compiler_params=_params("parallel", "parallel", "arbitrary"),
        name="diff_attn",
    )(lq, sub_g, proj3, proj3, proj3)


NA_QROWS = 8
NA_KROWS = 16
NA_TQ = NA_QROWS * GRID_W
NA_TK = NA_KROWS * GRID_W


def _na_table_body(b_ref, o_ref, *, rows):
    h = pl.program_id(0)
    variant = pl.program_id(1)
    qc = lax.broadcasted_iota(jnp.int32, (GRID_W, GRID_W), 0)
    kc = lax.broadcasted_iota(jnp.int32, (GRID_W, GRID_W), 1)
    cs = jnp.clip(qc - WIN_W // 2, 0, GRID_W - WIN_W)
    col_ok = (kc >= cs) & (kc < cs + WIN_W)
    dcol = kc - qc + (WIN_W - 1)
    masked = jnp.full((GRID_W, GRID_W), NEG_BIG, F32)
    col_tab = []
    for dr in range(2 * MAX_WIN_H - 1):
        t = masked
        for dc in range(2 * WIN_W - 1):
            t = jnp.where(dcol == dc, b_ref[h, dr, dc], t)
        col_tab.append(jnp.where(col_ok, t, NEG_BIG))

    blocks = ((0, 0), (NA_QROWS, NA_QROWS - MAX_WIN_H // 2), (rows - NA_QROWS, rows - NA_KROWS))
    for vi, (r0, kb) in enumerate(blocks):
        @pl.when(variant == vi)
        def _(r0=r0, kb=kb):
            for qr in range(NA_QROWS):
                r = r0 + qr
                rs = min(max(r - MAX_WIN_H // 2, 0), rows - MAX_WIN_H)
                for kp in range(NA_KROWS // 2):
                    halves = []
                    for kr in (2 * kp, 2 * kp + 1):
                        ka = kb + kr
                        inside = rs <= ka < rs + MAX_WIN_H
                        halves.append(col_tab[ka - r + MAX_WIN_H - 1] if inside else masked)
                    o_ref[qr * GRID_W:(qr + 1) * GRID_W, kp * LANES:(kp + 1) * LANES] = (
                        jnp.concatenate(halves, axis=1))


def _na_bias_table(bias, rows):
    nh = bias.shape[0]
    return pl.pallas_call(
        functools.partial(_na_table_body, rows=rows),
        grid=(nh, 3),
        in_specs=[pl.BlockSpec(memory_space=pltpu.SMEM)],
        out_specs=pl.BlockSpec((None, None, NA_TQ, NA_TK), lambda h, v: (h, v, 0, 0)),
        out_shape=jax.ShapeDtypeStruct((nh, 3, NA_TQ, NA_TK), F32),
        compiler_params=_params("parallel", "arbitrary"),
        name="na_bias_table",
    )(bias)


def _na_body(q_ref, k_ref, v_ref, b_ref, o_ref, *, rows):
    i = pl.program_id(2)
    kb = jnp.clip(i * NA_QROWS - MAX_WIN_H // 2, 0, rows - NA_KROWS)
    start = pl.multiple_of(kb * GRID_W, GRID_W)
    k = k_ref[pl.ds(start, NA_TK), :]
    v = v_ref[pl.ds(start, NA_TK), :]
    s = _nt_dot(q_ref[...], k) * ATTN_SCALE + b_ref[...]
    e = jnp.exp(s - jnp.max(s, axis=-1, keepdims=True))
    p = e * (1.0 / jnp.sum(e, axis=-1, keepdims=True))
    o_ref[...] = jnp.dot(p.astype(BF16), v, preferred_element_type=F32).astype(o_ref.dtype)


def _na_attention(proj3, bias_tab):
    B, S, _ = proj3.shape
    rows = S // GRID_W
    nblk = rows // NA_QROWS

    def bias_map(b, h, i):
        return (h, jnp.where(i == 0, 0, jnp.where(i == nblk - 1, 2, 1)), 0, 0)

    return pl.pallas_call(
        functools.partial(_na_body, rows=rows),
        grid=(B, N_NA_HEADS, nblk),
        in_specs=[
            pl.BlockSpec((None, NA_TQ, HEAD_DIM), lambda b, h, i: (b, i, COL_NQ + h)),
            pl.BlockSpec((None, S, HEAD_DIM), lambda b, h, i: (b, 0, COL_NK + h)),
            pl.BlockSpec((None, S, HEAD_DIM), lambda b, h, i: (b, 0, COL_NV + h)),
            pl.BlockSpec((None, None, NA_TQ, NA_TK), bias_map),
        ],
        out_specs=pl.BlockSpec((None, NA_TQ, HEAD_DIM), lambda b, h, i: (b, i, h)),
        out_shape=jax.ShapeDtypeStruct((B, S, NA_WIDTH), BF16),
        compiler_params=_params("parallel", "parallel", "arbitrary"),
        name="na_attn",
    )(proj3, proj3, proj3, bias_tab)


def _memkv_body(m_ref, g_ref, w_ref, gk_ref, k_ref, v_ref):
    hm = _rms(m_ref[...], g_ref[...]).astype(BF16)
    kv = jnp.dot(hm, w_ref[...], preferred_element_type=F32)
    for h in range(N_MEM_HEADS):
        sl = slice(h * HEAD_DIM, (h + 1) * HEAD_DIM)
        k_ref[:, sl] = _rms(kv[:, sl], gk_ref[...]).astype(BF16)
    v_ref[...] = kv[:, MEM_WIDTH:].astype(BF16)


def _mem_kv(mem, g, w_bf, gk):
    B, M, D = mem.shape
    return pl.pallas_call(
        _memkv_body,
        grid=(B,),
        in_specs=[
            pl.BlockSpec((None, M, D), lambda b: (b, 0, 0)),
            pl.BlockSpec((1, D), lambda b: (0, 0)),
            pl.BlockSpec((D, 2 * MEM_WIDTH), lambda b: (0, 0)),
            pl.BlockSpec((1, HEAD_DIM), lambda b: (0, 0)),
        ],
        out_specs=[pl.BlockSpec((None, M, MEM_WIDTH), lambda b: (b, 0, 0))] * 2,
        out_shape=[jax.ShapeDtypeStruct((B, M, MEM_WIDTH), BF16)] * 2,
        compiler_params=_params("parallel"),
        name="mem_kv",
    )(mem, g, w_bf, gk)


def _memattn_body(q_ref, k_ref, v_ref, o_ref):
    for h in range(N_MEM_HEADS):
        sl = slice(h * HEAD_DIM, (h + 1) * HEAD_DIM)
        s = _nt_dot(q_ref[:, sl], k_ref[:, sl]) * ATTN_SCALE
        e = jnp.exp(s - jnp.max(s, axis=-1, keepdims=True))
        p = e * (1.0 / jnp.sum(e, axis=-1, keepdims=True))
        o_ref[:, sl] = jnp.dot(p.astype(BF16), v_ref[:, sl], preferred_element_type=F32).astype(o_ref.dtype)


def _mem_attention(proj3, mk, mv, tq=512):
    B, S, _ = proj3.shape
    M = mk.shape[1]
    return pl.pallas_call(
        _memattn_body,
        grid=(B, S // tq),
        in_specs=[
            pl.BlockSpec((None, tq, MEM_WIDTH), lambda b, i: (b, i, COL_MQ // N_MEM_HEADS)),
            pl.BlockSpec((None, M, MEM_WIDTH), lambda b, i: (b, 0, 0)),
            pl.BlockSpec((None, M, MEM_WIDTH), lambda b, i: (b, 0, 0)),
        ],
        out_specs=pl.BlockSpec((None, tq, MEM_WIDTH), lambda b, i: (b, i, 0)),
        out_shape=jax.ShapeDtypeStruct((B, S, MEM_WIDTH), BF16),
        compiler_params=_params("parallel", "arbitrary"),
        name="mem_attn",
    )(proj3, mk, mv)


def _outproj_body(x_ref, od_ref, on_ref, om_ref, w_ref, g_ref, xn_ref, h_ref, ht_ref):
    mix = jnp.concatenate([od_ref[...], on_ref[...], om_ref[...]], axis=-1)
    xn = x_ref[...] + jnp.dot(mix, w_ref[...], preferred_element_type=F32)
    xn_ref[...] = xn
    h = _rms(xn, g_ref[...])
    h_ref[...] = h.astype(BF16)
    ht_ref[...] = h.T.astype(BF16)


def _out_proj(x2, od, on, om, w_bf, g, tm=256):
    T, D = x2.shape
    return pl.pallas_call(
        _outproj_body,
        grid=(T // tm,),
        in_specs=[
            pl.BlockSpec((tm, D), lambda i: (i, 0)),
            pl.BlockSpec((tm, DIFF_WIDTH), lambda i: (i, 0)),
            pl.BlockSpec((tm, NA_WIDTH), lambda i: (i, 0)),
            pl.BlockSpec((tm, MEM_WIDTH), lambda i: (i, 0)),
            pl.BlockSpec((D, D), lambda i: (0, 0)),
            pl.BlockSpec((1, D), lambda i: (0, 0)),
        ],
        out_specs=[pl.BlockSpec((tm, D), lambda i: (i, 0))] * 2 + [pl.BlockSpec((D, tm), lambda i: (0, i))],
        out_shape=[jax.ShapeDtypeStruct((T, D), F32), jax.ShapeDtypeStruct((T, D), BF16),
                   jax.ShapeDtypeStruct((D, T), BF16)],
        compiler_params=_params("parallel"),
        name="out_proj",
    )(x2, od, on, om, w_bf, g)


KH = N_KEYS * PEER_HEADS


def _sort_pairs(n):
    pairs = []

    def merge(lo, hi, r):
        step = r * 2
        if step < hi - lo:
            merge(lo, hi, step)
            merge(lo + r, hi, step)
            pairs.extend((i, i + r) for i in range(lo + r, hi - r, step))
        else:
            pairs.append((lo, lo + r))

    def sort(lo, hi):
        if hi > lo:
            mid = lo + (hi - lo) // 2
            sort(lo, mid)
            sort(mid + 1, hi)
            merge(lo, hi, 1)

    sort(0, n - 1)
    return pairs


_SORT16 = _sort_pairs(PEER_TOPK)


def _exchange(v, i, j):
    v[i], v[j] = jnp.maximum(v[i], v[j]), jnp.minimum(v[i], v[j])


def _sorted_desc(v):
    v = list(v)
    for i, j in _SORT16:
        _exchange(v, i, j)
    return v


def _merge_top(a, b):
    k = PEER_TOPK
    c = list(a)
    for i in range(len(b)):
        c[k - 1 - i] = jnp.maximum(a[k - 1 - i], b[i])
    d = k // 2
    while d >= 1:
        for i in range(k):
            if i & d == 0:
                _exchange(c, i, i + d)
        d //= 2
    return c


def _top16_of_keys(s_ref):
    best = None
    for g in range(N_KEYS // PEER_TOPK):
        grp = _sorted_desc([s_ref[(g * PEER_TOPK + r) * PEER_HEADS:(g * PEER_TOPK + r + 1) * PEER_HEADS, :]
                            for r in range(PEER_TOPK)])
        best = grp if best is None else _merge_top(best, grp)
    return best


def _router_body(h_ref, w_ref, kexp_ref, s1_ref, s2_ref, b_ref, tau_ref, mlz_ref, se_ref):
    tr = h_ref.shape[0]
    q = jnp.dot(h_ref[...], w_ref[...], preferred_element_type=F32).astype(BF16)
    for c in range(2):
        s = _nt_dot(kexp_ref[c], q[:, c * KH:(c + 1) * KH]) * LOG2E
        if c == 0:
            s1_ref[...] = s
        for lc in range(tr // LANES):
            se_ref[c, lc] = s[:, lc * LANES:(lc + 1) * LANES]

    k = PEER_TOPK
    for lc in range(tr // LANES):
        lanes = slice(lc * LANES, (lc + 1) * LANES)
        for h in range(PEER_HEADS):
            s2_ref[h, :, lanes] = se_ref[1, lc, pl.ds(h, N_KEYS, stride=PEER_HEADS), :]
        a = _top16_of_keys(se_ref.at[0, lc])
        b = _top16_of_keys(se_ref.at[1, lc])
        best = [a[0] + b[r] for r in range(k)]
        lists = [[a[r1] + b[r2] for r2 in range(k // (r1 + 1))] for r1 in range(1, k // 2)]
        lists.append([a[r1] + b[0] for r1 in range(k // 2, k)])
        for lst in lists:
            best = _merge_top(best, lst)
        z = jnp.ones_like(best[0])
        for v in best[1:]:
            z = z + jnp.exp2(v - best[0])
        for r in range(k):
            b_ref[r, :, lanes] = b[r]
        tau_ref[:, lanes] = best[k - 1]
        mlz_ref[:, lanes] = best[0] + jnp.log2(z) + 1.0


def _router(h2, w_bf, kexp_bf, tr=256):
    T, D = h2.shape
    qw = w_bf.shape[1]
    return pl.pallas_call(
        _router_body,
        grid=(T // tr,),
        in_specs=[
            pl.BlockSpec((tr, D), lambda i: (i, 0)),
            pl.BlockSpec((D, qw), lambda i: (0, 0)),
            pl.BlockSpec((2, KH, KH), lambda i: (0, 0, 0)),
        ],
        out_specs=[
            pl.BlockSpec((KH, tr), lambda i: (0, i)),
            pl.BlockSpec((PEER_HEADS, N_KEYS, tr), lambda i: (0, 0, i)),
            pl.BlockSpec((PEER_TOPK, PEER_HEADS, tr), lambda i: (0, 0, i)),
            pl.BlockSpec((PEER_HEADS, tr), lambda i: (0, i)),
            pl.BlockSpec((PEER_HEADS, tr), lambda i: (0, i)),
        ],
        out_shape=[
            jax.ShapeDtypeStruct((KH, T), F32),
            jax.ShapeDtypeStruct((PEER_HEADS, N_KEYS, T), F32),
            jax.ShapeDtypeStruct((PEER_TOPK, PEER_HEADS, T), F32),
            jax.ShapeDtypeStruct((PEER_HEADS, T), F32),
            jax.ShapeDtypeStruct((PEER_HEADS, T), F32),
        ],
        scratch_shapes=[pltpu.VMEM((2, tr // LANES, KH, LANES), F32)],
        compiler_params=_params("parallel"),
        name="peer_router",
    )(h2, w_bf, kexp_bf)


def _router_weights(w_pq, keys):
    D = w_pq.shape[0]
    wp = w_pq.reshape(D, PEER_HEADS, 2, N_KEYS).transpose(0, 2, 1, 3).reshape(D, 2 * KH)
    kt = keys.transpose(1, 2, 0, 3)
    same_head = jnp.eye(PEER_HEADS, dtype=bool)[None, None, :, :, None]
    kexp = jnp.where(same_head, kt[:, :, :, None, :], 0.0).reshape(2, KH, KH)
    return wp.astype(BF16), kexp.astype(BF16)


PEER_TC = 128
PEER_TR = 64
PEER_SUB = 256


def _peer_body(ht_ref, u_ref, vt_ref, s1_ref, s2_ref, b_ref, tau_ref, mlz_ref, x_ref, o_ref,
               acc_ref, thr_ref, off_ref, g_ref, *a_refs):
    j = pl.program_id(1)
    nsub = len(a_refs)
    tt = ht_ref.shape[1]
    n_i1 = s1_ref.shape[0]
    keys_per_sub = PEER_SUB // N_KEYS
    sub8 = PEER_TR // 8

    @pl.when(j == 0)
    def _():
        acc_ref[...] = jnp.zeros_like(acc_ref)

    for il in range(n_i1):
        for tc in range(tt // PEER_TC):
            lanes = slice(tc * PEER_TC, (tc + 1) * PEER_TC)
            s1 = s1_ref[il, :, lanes]
            tau = tau_ref[:, lanes]
            thr = jnp.full_like(s1, jnp.inf)
            for r in range(PEER_TOPK):
                b = b_ref[r, :, lanes]
                thr = jnp.where(s1 + b >= tau, b, thr)
            off = s1 - mlz_ref[:, lanes]
            for h in range(PEER_HEADS):
                row = il * PEER_HEADS + h
                thr_ref[row, :, lanes] = jnp.broadcast_to(thr[h:h + 1, :], (8, PEER_TC))
                off_ref[row, :, lanes] = jnp.broadcast_to(off[h:h + 1, :], (8, PEER_TC))

    def scores(k):
        a_refs[k][...] = jnp.dot(u_ref[k * PEER_SUB:(k + 1) * PEER_SUB, :], ht_ref[...],
                                 preferred_element_type=F32)

    def gate(k):
        a_ref = a_refs[k]
        for il in range(keys_per_sub):
            i1 = k * keys_per_sub + il
            for tc in range(tt // PEER_TC):
                lanes = slice(tc * PEER_TC, (tc + 1) * PEER_TC)
                for r0 in range(0, N_KEYS, PEER_TR):
                    w = None
                    for h in range(PEER_HEADS):
                        row = i1 * PEER_HEADS + h
                        s2 = s2_ref[h, r0:r0 + PEER_TR, lanes].reshape(sub8, 8, PEER_TC)
                        wh = jnp.where(s2 >= thr_ref[row, :, lanes][None],
                                       jnp.exp2(s2 + off_ref[row, :, lanes][None]), 0.0)
                        w = wh if w is None else w + wh
                    a = a_ref[il * N_KEYS + r0:il * N_KEYS + r0 + PEER_TR, lanes].reshape(sub8, 8, PEER_TC)
                    g = w * (a * (1.0 + lax.erf(a * SQRT_HALF)))
                    g_ref[i1 * N_KEYS + r0:i1 * N_KEYS + r0 + PEER_TR, lanes] = (
                        g.reshape(PEER_TR, PEER_TC).astype(BF16))

    for k in range(nsub):
        scores(k)
        gate(k)
    acc_ref[...] += jnp.dot(vt_ref[...], g_ref[...], preferred_element_type=F32)

    @pl.when(j == pl.num_programs(1) - 1)
    def _():
        o_ref[...] = x_ref[...] + acc_ref[...].T


def _peer_dense(xn, ht, u_all, vt_all, layer, s1, s2, b2, tau, mlz, tt=512, te=1024):
    D, T = ht.shape
    E = u_all.shape[1]
    n_i1 = te // N_KEYS
    nsub = te // PEER_SUB
    return pl.pallas_call(
        _peer_body,
        grid=(T // tt, E // te),
        in_specs=[
            pl.BlockSpec((D, tt), lambda i, j: (0, i)),
            pl.BlockSpec((None, te, D), lambda i, j: (layer, j, 0)),
            pl.BlockSpec((None, D, te), lambda i, j: (layer, 0, j)),
            pl.BlockSpec((n_i1, PEER_HEADS, tt), lambda i, j: (j, 0, i)),
            pl.BlockSpec((PEER_HEADS, N_KEYS, tt), lambda i, j: (0, 0, i)),
            pl.BlockSpec((PEER_TOPK, PEER_HEADS, tt), lambda i, j: (0, 0, i)),
            pl.BlockSpec((PEER_HEADS, tt), lambda i, j: (0, i)),
            pl.BlockSpec((PEER_HEADS, tt), lambda i, j: (0, i)),
            pl.BlockSpec((tt, D), lambda i, j: (i, 0)),
        ],
        out_specs=pl.BlockSpec((tt, D), lambda i, j: (i, 0)),
        out_shape=jax.ShapeDtypeStruct((T, D), F32),
        scratch_shapes=([pltpu.VMEM((D, tt), F32)] + [pltpu.VMEM((n_i1 * PEER_HEADS, 8, tt), F32)] * 2
                        + [pltpu.VMEM((te, tt), BF16)] + [pltpu.VMEM((PEER_SUB, tt), F32)] * nsub),
        compiler_params=_params("parallel", "arbitrary"),
        name="peer_dense",
    )(ht, u_all, vt_all, s1.reshape(N_KEYS, PEER_HEADS, T), s2, b2, tau, mlz, xn)


def _rope_tables(seq):
    inv = ROPE_THETA ** (-jnp.arange(0, ROT_DIM, 2, dtype=jnp.float32) / ROT_DIM)
    ang = jnp.arange(seq, dtype=jnp.float32)[:, None] * inv[None, :]
    cos, sin = jnp.cos(ang), jnp.sin(ang)
    rest = HEAD_DIM - ROT_DIM
    zeros_h = jnp.zeros((seq, ROT_HALF), F32)
    cf = jnp.concatenate([cos, cos, jnp.ones((seq, rest), F32)], axis=-1)
    sa = jnp.concatenate([-sin, zeros_h, jnp.zeros((seq, rest), F32)], axis=-1)
    sb = jnp.concatenate([zeros_h, sin, jnp.zeros((seq, rest), F32)], axis=-1)
    return cf, sa, sb


def kernel(x, mem, attn_norm, w_in, qk_gain, lambda_qk, subln_gain, na_bias, mem_norm, w_mem_kv,
           w_out, ffn_norm, w_pq, peer_keys, peer_u, peer_v):
    B, S, D = x.shape
    depth = w_in.shape[0]
    T = B * S
    cf, sa, sb = _rope_tables(S)
    x2 = x.reshape(T, D)
    u_all = peer_u.astype(BF16)
    vt_all = peer_v.transpose(0, 2, 1).astype(BF16)
    for l in range(depth):
        lam_init = 0.8 - 0.6 * math.exp(-0.3 * l)
        proj = _in_proj(x2, attn_norm[l][None], w_in[l].astype(BF16), qk_gain[l], cf, sa, sb, S)
        proj3 = proj.reshape(B, S, IN_WIDTH)
        o_diff = _diff_attention(proj3, lambda_qk[l], subln_gain[l][None], lam_init)
        o_na = _na_attention(proj3, _na_bias_table(na_bias[l], S // GRID_W))
        mk, mv = _mem_kv(mem, mem_norm[l][None], w_mem_kv[l].astype(BF16), qk_gain[l][5:6])
        o_mem = _mem_attention(proj3, mk, mv)
        xn, h2, h2t = _out_proj(x2, o_diff.reshape(T, DIFF_WIDTH), o_na.reshape(T, NA_WIDTH),
                                o_mem.reshape(T, MEM_WIDTH), w_out[l].astype(BF16), ffn_norm[l][None])
        s1, s2, b2, tau, mlz = _router(h2, *_router_weights(w_pq[l], peer_keys[l]))
        x2 = _peer_dense(xn, h2t, u_all, vt_all, l, s1, s2, b2, tau, mlz)
    return x2.reshape(B, S, D)
```

```python
import functools
import math

import jax
import jax.numpy as jnp
from jax import lax
from jax.experimental import pallas as pl
from jax.experimental.pallas import tpu as pltpu

F32 = jnp.float32
BF16 = jnp.bfloat16

HEAD_DIM = 128
N_DIFF_HEADS = 4
N_NA_HEADS = 4
N_MEM_HEADS = 4
DIFF_V_DIM = 2 * HEAD_DIM
DIFF_WIDTH = N_DIFF_HEADS * DIFF_V_DIM
NA_WIDTH = N_NA_HEADS * HEAD_DIM
MEM_WIDTH = N_MEM_HEADS * HEAD_DIM
IN_WIDTH = 3 * DIFF_WIDTH + 3 * NA_WIDTH + MEM_WIDTH
ROPE_THETA = 500000.0
ROT_DIM = HEAD_DIM // 4
ROT_HALF = ROT_DIM // 2
GRID_W = 64
MAX_WIN_H = 8
WIN_W = 16
PEER_HEADS = 8
N_KEYS = 128
PEER_TOPK = 16
EPS = 1e-6
ATTN_SCALE = HEAD_DIM ** -0.5
NEG_BIG = -1e30
SQRT_HALF = math.sqrt(0.5)
LOG2E = math.log2(math.e)

LANES = 128
VMEM_LIMIT = 56 * 1024 * 1024

COL_DQ, COL_DK, COL_DV = 0, 8, 16
COL_NQ, COL_NK, COL_NV, COL_MQ = 24, 28, 32, 36


def _params(*sem):
    return pltpu.CompilerParams(dimension_semantics=sem, vmem_limit_bytes=VMEM_LIMIT)


def _rms(p, g):
    ms = jnp.mean(p * p, axis=-1, keepdims=True)
    return p * lax.rsqrt(ms + EPS) * g


def _nt_dot(a, b):
    return lax.dot_general(a, b, (((1,), (1,)), ((), ())), preferred_element_type=F32)


IN_TN = 1024


def _inproj_body(x_ref, g_ref, w_ref, gq_ref, cf_ref, sa_ref, sb_ref, o_ref, h_ref):
    j = pl.program_id(1)

    @pl.when(j == 0)
    def _():
        h_ref[...] = _rms(x_ref[...], g_ref[...]).astype(BF16)

    acc = jnp.dot(h_ref[...], w_ref[...], preferred_element_type=F32)
    ngrp = IN_TN // HEAD_DIM

    def grp(g):
        return acc[:, g * HEAD_DIM:(g + 1) * HEAD_DIM]

    def put(g, val):
        o_ref[:, g * HEAD_DIM:(g + 1) * HEAD_DIM] = val.astype(BF16)

    def rope(p):
        return (p * cf_ref[...] + pltpu.roll(p, HEAD_DIM - ROT_HALF, 1) * sa_ref[...]
                + pltpu.roll(p, ROT_HALF, 1) * sb_ref[...])

    for jj in (0, 1):
        @pl.when(j == jj)
        def _(jj=jj):
            gain = gq_ref[jj:jj + 1, :]
            for g in range(ngrp):
                put(g, rope(_rms(grp(g), gain)))

    @pl.when(j == 2)
    def _():
        o_ref[...] = acc.astype(BF16)

    @pl.when(j == 3)
    def _():
        for g in range(ngrp):
            row = 2 if g < ngrp // 2 else 3
            put(g, _rms(grp(g), gq_ref[row:row + 1, :]))

    @pl.when(j == 4)
    def _():
        for g in range(ngrp):
            if g < ngrp // 2:
                put(g, grp(g))
            else:
                put(g, _rms(grp(g), gq_ref[4:5, :]))


def _in_proj(x2, g, w_all, layer, gq, cf, sa, sb, seq, tm=1024):
    T, D = x2.shape
    nseq = seq // tm
    return pl.pallas_call(
        _inproj_body,
        grid=(T // tm, IN_WIDTH // IN_TN),
        in_specs=[
            pl.BlockSpec((tm, D), lambda i, j: (i, 0)),
            pl.BlockSpec((1, D), lambda i, j: (0, 0)),
            pl.BlockSpec((None, D, IN_TN), lambda i, j: (layer, 0, j)),
            pl.BlockSpec((6, HEAD_DIM), lambda i, j: (0, 0)),
            pl.BlockSpec((tm, HEAD_DIM), lambda i, j: (i % nseq, 0)),
            pl.BlockSpec((tm, HEAD_DIM), lambda i, j: (i % nseq, 0)),
            pl.BlockSpec((tm, HEAD_DIM), lambda i, j: (i % nseq, 0)),
        ],
        out_specs=pl.BlockSpec((tm, IN_TN), lambda i, j: (i, j)),
        out_shape=jax.ShapeDtypeStruct((T, IN_WIDTH), BF16),
        scratch_shapes=[pltpu.VMEM((tm, D), BF16)],
        compiler_params=_params("parallel", "arbitrary"),
        name="in_proj",
    )(x2, g, w_all, gq, cf, sa, sb)


DIFF_SUBQ = 256


def _diff_body(lq_ref, g_ref, q_ref, k_ref, v_ref, o_ref, *, lam_init):
    lq = lq_ref[...]
    lam = (jnp.exp(jnp.sum(lq[0:1] * lq[1:2], axis=-1, keepdims=True))
           - jnp.exp(jnp.sum(lq[2:3] * lq[3:4], axis=-1, keepdims=True)) + lam_init)

    def attend(rows, c):
        q = q_ref[rows, c * HEAD_DIM:(c + 1) * HEAD_DIM]
        k = k_ref[:, c * HEAD_DIM:(c + 1) * HEAD_DIM]
        s = _nt_dot(q, k)
        e = jnp.exp2((s - jnp.max(s, axis=-1, keepdims=True)) * (ATTN_SCALE * LOG2E))
        l = jnp.sum(e, axis=-1, keepdims=True)
        return jnp.dot(e.astype(BF16), v_ref[...], preferred_element_type=F32), l

    for r0 in range(0, q_ref.shape[0], DIFF_SUBQ):
        rows = slice(r0, r0 + DIFF_SUBQ)
        o1, l1 = attend(rows, 0)
        o2, l2 = attend(rows, 1)
        o = o1 * (1.0 / l1) - o2 * (lam / l2)
        o_ref[rows, :] = (_rms(o, g_ref[...]) * (1.0 - lam_init)).astype(o_ref.dtype)


def _diff_attention(proj3, lq, sub_g, lam_init, tq=1024):
    B, S, _ = proj3.shape
    return pl.pallas_call(
        functools.partial(_diff_body, lam_init=lam_init),
        grid=(B, N_DIFF_HEADS, S // tq),
        in_specs=[
            pl.BlockSpec((4, HEAD_DIM), lambda b, h, i: (0, 0)),
            pl.BlockSpec((1, DIFF_V_DIM), lambda b, h, i: (0, 0)),
            pl.BlockSpec((None, tq, DIFF_V_DIM), lambda b, h, i: (b, i, COL_DQ // 2 + h)),
            pl.BlockSpec((None, S, DIFF_V_DIM), lambda b, h, i: (b, 0, COL_DK // 2 + h)),
            pl.BlockSpec((None, S, DIFF_V_DIM), lambda b, h, i: (b, 0, COL_DV // 2 + h)),
        ],
        out_specs=pl.BlockSpec((None, tq, DIFF_V_DIM), lambda b, h, i: (b, i, h)),
        out_shape=jax.ShapeDtypeStruct((B, S, DIFF_WIDTH), BF16),
        compiler_params=_params("parallel", "parallel", "arbitrary"),
        name="diff_attn",
    )(lq, sub_g, proj3, proj3, proj3)


NA_QROWS = 8
NA_KROWS = 16
NA_TQ = NA_QROWS * GRID_W
NA_TK = NA_KROWS * GRID_W


def _na_table_body(b_ref, o_ref, *, rows):
    h = pl.program_id(0)
    variant = pl.program_id(1)
    qc = lax.broadcasted_iota(jnp.int32, (GRID_W, GRID_W), 0)
    kc = lax.broadcasted_iota(jnp.int32, (GRID_W, GRID_W), 1)
    cs = jnp.clip(qc - WIN_W // 2, 0, GRID_W - WIN_W)
    col_ok = (kc >= cs) & (kc < cs + WIN_W)
    dcol = kc - qc + (WIN_W - 1)
    masked = jnp.full((GRID_W, GRID_W), NEG_BIG, F32)
    col_tab = []
    for dr in range(2 * MAX_WIN_H - 1):
        t = masked
        for dc in range(2 * WIN_W - 1):
            t = jnp.where(dcol == dc, b_ref[h, dr, dc] * LOG2E, t)
        col_tab.append(jnp.where(col_ok, t, NEG_BIG))

    blocks = ((0, 0), (NA_QROWS, NA_QROWS - MAX_WIN_H // 2), (rows - NA_QROWS, rows - NA_KROWS))
    for vi, (r0, kb) in enumerate(blocks):
        @pl.when(variant == vi)
        def _(r0=r0, kb=kb):
            for qr in range(NA_QROWS):
                r = r0 + qr
                rs = min(max(r - MAX_WIN_H // 2, 0), rows - MAX_WIN_H)
                for kp in range(NA_KROWS // 2):
                    halves = []
                    for kr in (2 * kp, 2 * kp + 1):
                        ka = kb + kr
                        inside = rs <= ka < rs + MAX_WIN_H
                        halves.append(col_tab[ka - r + MAX_WIN_H - 1] if inside else masked)
                    o_ref[qr * GRID_W:(qr + 1) * GRID_W, kp * LANES:(kp + 1) * LANES] = (
                        jnp.concatenate(halves, axis=1))


def _na_bias_table(bias, rows):
    nh = bias.shape[0]
    return pl.pallas_call(
        functools.partial(_na_table_body, rows=rows),
        grid=(nh, 3),
        in_specs=[pl.BlockSpec(memory_space=pltpu.SMEM)],
        out_specs=pl.BlockSpec((None, None, NA_TQ, NA_TK), lambda h, v: (h, v, 0, 0)),
        out_shape=jax.ShapeDtypeStruct((nh, 3, NA_TQ, NA_TK), F32),
        compiler_params=_params("parallel", "arbitrary"),
        name="na_bias_table",
    )(bias)


def _na_body(q_ref, k_ref, v_ref, b_ref, o_ref, *, rows):
    i = pl.program_id(2)
    kb = jnp.clip(i * NA_QROWS - MAX_WIN_H // 2, 0, rows - NA_KROWS)
    start = pl.multiple_of(kb * GRID_W, GRID_W)
    k = k_ref[pl.ds(start, NA_TK), :]
    v = v_ref[pl.ds(start, NA_TK), :]
    t = _nt_dot(q_ref[...], k) * (ATTN_SCALE * LOG2E) + b_ref[...]
    e = jnp.exp2(t - jnp.max(t, axis=-1, keepdims=True))
    p = e * (1.0 / jnp.sum(e, axis=-1, keepdims=True))
    o_ref[...] = jnp.dot(p.astype(BF16), v, preferred_element_type=F32).astype(o_ref.dtype)


def _na_attention(proj3, bias_tab):
    B, S, _ = proj3.shape
    rows = S // GRID_W
    nblk = rows // NA_QROWS

    def bias_map(b, h, i):
        return (h, jnp.where(i == 0, 0, jnp.where(i == nblk - 1, 2, 1)), 0, 0)

    return pl.pallas_call(
        functools.partial(_na_body, rows=rows),
        grid=(B, N_NA_HEADS, nblk),
        in_specs=[
            pl.BlockSpec((None, NA_TQ, HEAD_DIM), lambda b, h, i: (b, i, COL_NQ + h)),
            pl.BlockSpec((None, S, HEAD_DIM), lambda b, h, i: (b, 0, COL_NK + h)),
            pl.BlockSpec((None, S, HEAD_DIM), lambda b, h, i: (b, 0, COL_NV + h)),
            pl.BlockSpec((None, None, NA_TQ, NA_TK), bias_map),
        ],
        out_specs=pl.BlockSpec((None, NA_TQ, HEAD_DIM), lambda b, h, i: (b, i, h)),
        out_shape=jax.ShapeDtypeStruct((B, S, NA_WIDTH), BF16),
        compiler_params=_params("parallel", "parallel", "arbitrary"),
        name="na_attn",
    )(proj3, proj3, proj3, bias_tab)


def _memkv_body(m_ref, g_ref, w_ref, gk_ref, k_ref, v_ref):
    hm = _rms(m_ref[...], g_ref[...]).astype(BF16)
    kv = jnp.dot(hm, w_ref[...], preferred_element_type=F32)
    for h in range(N_MEM_HEADS):
        sl = slice(h * HEAD_DIM, (h + 1) * HEAD_DIM)
        k_ref[:, sl] = _rms(kv[:, sl], gk_ref[...]).astype(BF16)
    v_ref[...] = kv[:, MEM_WIDTH:].astype(BF16)


def _mem_kv(mem, g, w_bf, gk):
    B, M, D = mem.shape
    return pl.pallas_call(
        _memkv_body,
        grid=(B,),
        in_specs=[
            pl.BlockSpec((None, M, D), lambda b: (b, 0, 0)),
            pl.BlockSpec((1, D), lambda b: (0, 0)),
            pl.BlockSpec((D, 2 * MEM_WIDTH), lambda b: (0, 0)),
            pl.BlockSpec((1, HEAD_DIM), lambda b: (0, 0)),
        ],
        out_specs=[pl.BlockSpec((None, M, MEM_WIDTH), lambda b: (b, 0, 0))] * 2,
        out_shape=[jax.ShapeDtypeStruct((B, M, MEM_WIDTH), BF16)] * 2,
        compiler_params=_params("parallel"),
        name="mem_kv",
    )(mem, g, w_bf, gk)


def _memattn_body(q_ref, k_ref, v_ref, o_ref):
    for h in range(N_MEM_HEADS):
        sl = slice(h * HEAD_DIM, (h + 1) * HEAD_DIM)
        s = _nt_dot(q_ref[:, sl], k_ref[:, sl])
        e = jnp.exp2((s - jnp.max(s, axis=-1, keepdims=True)) * (ATTN_SCALE * LOG2E))
        p = e * (1.0 / jnp.sum(e, axis=-1, keepdims=True))
        o_ref[:, sl] = jnp.dot(p.astype(BF16), v_ref[:, sl], preferred_element_type=F32).astype(o_ref.dtype)


def _mem_attention(proj3, mk, mv, tq=512):
    B, S, _ = proj3.shape
    M = mk.shape[1]
    return pl.pallas_call(
        _memattn_body,
        grid=(B, S // tq),
        in_specs=[
            pl.BlockSpec((None, tq, MEM_WIDTH), lambda b, i: (b, i, COL_MQ // N_MEM_HEADS)),
            pl.BlockSpec((None, M, MEM_WIDTH), lambda b, i: (b, 0, 0)),
            pl.BlockSpec((None, M, MEM_WIDTH), lambda b, i: (b, 0, 0)),
        ],
        out_specs=pl.BlockSpec((None, tq, MEM_WIDTH), lambda b, i: (b, i, 0)),
        out_shape=jax.ShapeDtypeStruct((B, S, MEM_WIDTH), BF16),
        compiler_params=_params("parallel", "arbitrary"),
        name="mem_attn",
    )(proj3, mk, mv)


def _outproj_body(x_ref, od_ref, on_ref, om_ref, w_ref, g_ref, xn_ref, h_ref, ht_ref):
    mix = jnp.concatenate([od_ref[...], on_ref[...], om_ref[...]], axis=-1)
    xn = x_ref[...] + jnp.dot(mix, w_ref[...], preferred_element_type=F32)
    xn_ref[...] = xn
    h = _rms(xn, g_ref[...])
    h_ref[...] = h.astype(BF16)
    ht_ref[...] = h.T.astype(BF16)


def _out_proj(x2, od, on, om, w_bf, g, tm=256):
    T, D = x2.shape
    return pl.pallas_call(
        _outproj_body,
        grid=(T // tm,),
        in_specs=[
            pl.BlockSpec((tm, D), lambda i: (i, 0)),
            pl.BlockSpec((tm, DIFF_WIDTH), lambda i: (i, 0)),
            pl.BlockSpec((tm, NA_WIDTH), lambda i: (i, 0)),
            pl.BlockSpec((tm, MEM_WIDTH), lambda i: (i, 0)),
            pl.BlockSpec((D, D), lambda i: (0, 0)),
            pl.BlockSpec((1, D), lambda i: (0, 0)),
        ],
        out_specs=[pl.BlockSpec((tm, D), lambda i: (i, 0))] * 2 + [pl.BlockSpec((D, tm), lambda i: (0, i))],
        out_shape=[jax.ShapeDtypeStruct((T, D), F32), jax.ShapeDtypeStruct((T, D), BF16),
                   jax.ShapeDtypeStruct((D, T), BF16)],
        compiler_params=_params("parallel"),
        name="out_proj",
    )(x2, od, on, om, w_bf, g)


KH = N_KEYS * PEER_HEADS


def _sort_pairs(n):
    pairs = []

    def merge(lo, hi, r):
        step = r * 2
        if step < hi - lo:
            merge(lo, hi, step)
            merge(lo + r, hi, step)
            pairs.extend((i, i + r) for i in range(lo + r, hi - r, step))
        else:
            pairs.append((lo, lo + r))

    def sort(lo, hi):
        if hi > lo:
            mid = lo + (hi - lo) // 2
            sort(lo, mid)
            sort(mid + 1, hi)
            merge(lo, hi, 1)

    sort(0, n - 1)
    return pairs


_SORT16 = _sort_pairs(PEER_TOPK)


def _exchange(v, i, j):
    v[i], v[j] = jnp.maximum(v[i], v[j]), jnp.minimum(v[i], v[j])


def _sorted_desc(v):
    v = list(v)
    for i, j in _SORT16:
        _exchange(v, i, j)
    return v


def _merge_top(a, b):
    k = PEER_TOPK
    c = list(a)
    for i in range(len(b)):
        c[k - 1 - i] = jnp.maximum(a[k - 1 - i], b[i])
    d = k // 2
    while d >= 1:
        for i in range(k):
            if i & d == 0:
                _exchange(c, i, i + d)
        d //= 2
    return c


def _top16_of_keys(s_ref):
    best = None
    for g in range(N_KEYS // PEER_TOPK):
        grp = _sorted_desc([s_ref[(g * PEER_TOPK + r) * PEER_HEADS:(g * PEER_TOPK + r + 1) * PEER_HEADS, :]
                            for r in range(PEER_TOPK)])
        best = grp if best is None else _merge_top(best, grp)
    return best


def _router_body(h_ref, w_ref, kexp_ref, s1_ref, s2_ref, b_ref, tau_ref, mlz_ref, se_ref):
    tr = h_ref.shape[0]
    q = jnp.dot(h_ref[...], w_ref[...], preferred_element_type=F32).astype(BF16)
    for c in range(2):
        s = _nt_dot(kexp_ref[c], q[:, c * KH:(c + 1) * KH]) * LOG2E
        if c == 0:
            s1_ref[...] = s
        for lc in range(tr // LANES):
            se_ref[c, lc] = s[:, lc * LANES:(lc + 1) * LANES]

    k = PEER_TOPK
    for lc in range(tr // LANES):
        lanes = slice(lc * LANES, (lc + 1) * LANES)
        for h in range(PEER_HEADS):
            s2_ref[h, :, lanes] = se_ref[1, lc, pl.ds(h, N_KEYS, stride=PEER_HEADS), :]
        a = _top16_of_keys(se_ref.at[0, lc])
        b = _top16_of_keys(se_ref.at[1, lc])
        best = [a[0] + b[r] for r in range(k)]
        lists = [[a[r1] + b[r2] for r2 in range(k // (r1 + 1))] for r1 in range(1, k // 2)]
        lists.append([a[r1] + b[0] for r1 in range(k // 2, k)])
        for lst in lists:
            best = _merge_top(best, lst)
        z = jnp.ones_like(best[0])
        for v in best[1:]:
            z = z + jnp.exp2(v - best[0])
        for r in range(k):
            b_ref[r, :, lanes] = b[r]
        tau_ref[:, lanes] = best[k - 1]
        mlz_ref[:, lanes] = best[0] + jnp.log2(z) + 1.0


def _router(h2, w_bf, kexp_bf, tr=256):
    T, D = h2.shape
    qw = w_bf.shape[1]
    return pl.pallas_call(
        _router_body,
        grid=(T // tr,),
        in_specs=[
            pl.BlockSpec((tr, D), lambda i: (i, 0)),
            pl.BlockSpec((D, qw), lambda i: (0, 0)),
            pl.BlockSpec((2, KH, KH), lambda i: (0, 0, 0)),
        ],
        out_specs=[
            pl.BlockSpec((KH, tr), lambda i: (0, i)),
            pl.BlockSpec((PEER_HEADS, N_KEYS, tr), lambda i: (0, 0, i)),
            pl.BlockSpec((PEER_TOPK, PEER_HEADS, tr), lambda i: (0, 0, i)),
            pl.BlockSpec((PEER_HEADS, tr), lambda i: (0, i)),
            pl.BlockSpec((PEER_HEADS, tr), lambda i: (0, i)),
        ],
        out_shape=[
            jax.ShapeDtypeStruct((KH, T), F32),
            jax.ShapeDtypeStruct((PEER_HEADS, N_KEYS, T), F32),
            jax.ShapeDtypeStruct((PEER_TOPK, PEER_HEADS, T), F32),
            jax.ShapeDtypeStruct((PEER_HEADS, T), F32),
            jax.ShapeDtypeStruct((PEER_HEADS, T), F32),
        ],
        scratch_shapes=[pltpu.VMEM((2, tr // LANES, KH, LANES), F32)],
        compiler_params=_params("parallel"),
        name="peer_router",
    )(h2, w_bf, kexp_bf)


def _router_weights(w_pq, keys):
    D = w_pq.shape[0]
    wp = w_pq.reshape(D, PEER_HEADS, 2, N_KEYS).transpose(0, 2, 1, 3).reshape(D, 2 * KH)
    kt = keys.transpose(1, 2, 0, 3)
    same_head = jnp.eye(PEER_HEADS, dtype=bool)[None, None, :, :, None]
    kexp = jnp.where(same_head, kt[:, :, :, None, :], 0.0).reshape(2, KH, KH)
    return wp.astype(BF16), kexp.astype(BF16)


PEER_TC = 128
PEER_TR = 64
PEER_SUB = 256


def _peer_body(ht_ref, u_ref, vt_ref, s1_ref, s2_ref, b_ref, tau_ref, mlz_ref, x_ref, o_ref,
               acc_ref, thr_ref, off_ref, g_ref, *a_refs):
    j = pl.program_id(1)
    nsub = len(a_refs)
    tt = ht_ref.shape[1]
    n_i1 = s1_ref.shape[0]
    keys_per_sub = PEER_SUB // N_KEYS
    sub8 = PEER_TR // 8

    @pl.when(j == 0)
    def _():
        acc_ref[...] = jnp.zeros_like(acc_ref)

    for il in range(n_i1):
        for tc in range(tt // PEER_TC):
            lanes = slice(tc * PEER_TC, (tc + 1) * PEER_TC)
            s1 = s1_ref[il, :, lanes]
            tau = tau_ref[:, lanes]
            thr = jnp.full_like(s1, jnp.inf)
            for r in range(PEER_TOPK):
                b = b_ref[r, :, lanes]
                thr = jnp.where(s1 + b >= tau, b, thr)
            off = s1 - mlz_ref[:, lanes]
            for h in range(PEER_HEADS):
                row = il * PEER_HEADS + h
                thr_ref[row, :, lanes] = jnp.broadcast_to(thr[h:h + 1, :], (8, PEER_TC))
                off_ref[row, :, lanes] = jnp.broadcast_to(off[h:h + 1, :], (8, PEER_TC))

    def scores(k):
        a_refs[k][...] = jnp.dot(u_ref[k * PEER_SUB:(k + 1) * PEER_SUB, :], ht_ref[...],
                                 preferred_element_type=F32)

    def gate(k):
        a_ref = a_refs[k]
        for il in range(keys_per_sub):
            i1 = k * keys_per_sub + il
            for tc in range(tt // PEER_TC):
                lanes = slice(tc * PEER_TC, (tc + 1) * PEER_TC)
                for r0 in range(0, N_KEYS, PEER_TR):
                    w = None
                    for h in range(PEER_HEADS):
                        row = i1 * PEER_HEADS + h
                        s2 = s2_ref[h, r0:r0 + PEER_TR, lanes].reshape(sub8, 8, PEER_TC)
                        wh = jnp.where(s2 >= thr_ref[row, :, lanes][None],
                                       jnp.exp2(s2 + off_ref[row, :, lanes][None]), 0.0)
                        w = wh if w is None else w + wh
                    a = a_ref[il * N_KEYS + r0:il * N_KEYS + r0 + PEER_TR, lanes].reshape(sub8, 8, PEER_TC)
                    g = w * (a * (1.0 + lax.erf(a * SQRT_HALF)))
                    g_ref[i1 * N_KEYS + r0:i1 * N_KEYS + r0 + PEER_TR, lanes] = (
                        g.reshape(PEER_TR, PEER_TC).astype(BF16))

    for k in range(nsub):
        scores(k)
        gate(k)
    acc_ref[...] += jnp.dot(vt_ref[...], g_ref[...], preferred_element_type=F32)

    @pl.when(j == pl.num_programs(1) - 1)
    def _():
        o_ref[...] = x_ref[...] + acc_ref[...].T


def _peer_dense(xn, ht, u_all, vt_all, layer, s1, s2, b2, tau, mlz, tt=512, te=1024):
    D, T = ht.shape
    E = u_all.shape[1]
    n_i1 = te // N_KEYS
    nsub = te // PEER_SUB
    return pl.pallas_call(
        _peer_body,
        grid=(T // tt, E // te),
        in_specs=[
            pl.BlockSpec((D, tt), lambda i, j: (0, i)),
            pl.BlockSpec((None, te, D), lambda i, j: (layer, j, 0)),
            pl.BlockSpec((None, D, te), lambda i, j: (layer, 0, j)),
            pl.BlockSpec((n_i1, PEER_HEADS, tt), lambda i, j: (j, 0, i)),
            pl.BlockSpec((PEER_HEADS, N_KEYS, tt), lambda i, j: (0, 0, i)),
            pl.BlockSpec((PEER_TOPK, PEER_HEADS, tt), lambda i, j: (0, 0, i)),
            pl.BlockSpec((PEER_HEADS, tt), lambda i, j: (0, i)),
            pl.BlockSpec((PEER_HEADS, tt), lambda i, j: (0, i)),
            pl.BlockSpec((tt, D), lambda i, j: (i, 0)),
        ],
        out_specs=pl.BlockSpec((tt, D), lambda i, j: (i, 0)),
        out_shape=jax.ShapeDtypeStruct((T, D), F32),
        scratch_shapes=([pltpu.VMEM((D, tt), F32)] + [pltpu.VMEM((n_i1 * PEER_HEADS, 8, tt), F32)] * 2
                        + [pltpu.VMEM((te, tt), BF16)] + [pltpu.VMEM((PEER_SUB, tt), F32)] * nsub),
        compiler_params=_params("parallel", "arbitrary"),
        name="peer_dense",
    )(ht, u_all, vt_all, s1.reshape(N_KEYS, PEER_HEADS, T), s2, b2, tau, mlz, xn)


def _rope_tables(seq):
    inv = ROPE_THETA ** (-jnp.arange(0, ROT_DIM, 2, dtype=jnp.float32) / ROT_DIM)
    ang = jnp.arange(seq, dtype=jnp.float32)[:, None] * inv[None, :]
    cos, sin = jnp.cos(ang), jnp.sin(ang)
    rest = HEAD_DIM - ROT_DIM
    zeros_h = jnp.zeros((seq, ROT_HALF), F32)
    cf = jnp.concatenate([cos, cos, jnp.ones((seq, rest), F32)], axis=-1)
    sa = jnp.concatenate([-sin, zeros_h, jnp.zeros((seq, rest), F32)], axis=-1)
    sb = jnp.concatenate([zeros_h, sin, jnp.zeros((seq, rest), F32)], axis=-1)
    return cf, sa, sb


def kernel(x, mem, attn_norm, w_in, qk_gain, lambda_qk, subln_gain, na_bias, mem_norm, w_mem_kv,
           w_out, ffn_norm, w_pq, peer_keys, peer_u, peer_v):
    B, S, D = x.shape
    depth = w_in.shape[0]
    T = B * S
    cf, sa, sb = _rope_tables(S)
    x2 = x.reshape(T, D)
    w_in_all = w_in.astype(BF16)
    u_all = peer_u.astype(BF16)
    vt_all = peer_v.transpose(0, 2, 1).astype(BF16)
    for l in range(depth):
        lam_init = 0.8 - 0.6 * math.exp(-0.3 * l)
        proj = _in_proj(x2, attn_norm[l][None], w_in_all, l, qk_gain[l], cf, sa, sb, S)
        proj3 = proj.reshape(B, S, IN_WIDTH)
        o_diff = _diff_attention(proj3, lambda_qk[l], subln_gain[l][None], lam_init)
        o_na = _na_attention(proj3, _na_bias_table(na_bias[l], S // GRID_W))
        mk, mv = _mem_kv(mem, mem_norm[l][None], w_mem_kv[l].astype(BF16), qk_gain[l][5:6])
        o_mem = _mem_attention(proj3, mk, mv)
        xn, h2, h2t = _out_proj(x2, o_diff.reshape(T, DIFF_WIDTH), o_na.reshape(T, NA_WIDTH),
                                o_mem.reshape(T, MEM_WIDTH), w_out[l].astype(BF16), ffn_norm[l][None])
        s1, s2, b2, tau, mlz = _router(h2, *_router_weights(w_pq[l], peer_keys[l]))
        x2 = _peer_dense(xn, h2t, u_all, vt_all, l, s1, s2, b2, tau, mlz)
    return x2.reshape(B, S, D)
```

```python
import functools
import math

import jax
import jax.numpy as jnp
from jax import lax
from jax.experimental import pallas as pl
from jax.experimental.pallas import tpu as pltpu

F32 = jnp.float32
BF16 = jnp.bfloat16

HEAD_DIM = 128
N_DIFF_HEADS = 4
N_NA_HEADS = 4
N_MEM_HEADS = 4
DIFF_V_DIM = 2 * HEAD_DIM
DIFF_WIDTH = N_DIFF_HEADS * DIFF_V_DIM
NA_WIDTH = N_NA_HEADS * HEAD_DIM
MEM_WIDTH = N_MEM_HEADS * HEAD_DIM
IN_WIDTH = 3 * DIFF_WIDTH + 3 * NA_WIDTH + MEM_WIDTH
ROPE_THETA = 500000.0
ROT_DIM = HEAD_DIM // 4
ROT_HALF = ROT_DIM // 2
GRID_W = 64
MAX_WIN_H = 8
WIN_W = 16
PEER_HEADS = 8
N_KEYS = 128
PEER_TOPK = 16
EPS = 1e-6
ATTN_SCALE = HEAD_DIM ** -0.5
NEG_BIG = -1e30
SQRT_HALF = math.sqrt(0.5)
LOG2E = math.log2(math.e)

LANES = 128
VMEM_LIMIT = 56 * 1024 * 1024

COL_DQ, COL_DK, COL_DV = 0, 8, 16
COL_NQ, COL_NK, COL_NV, COL_MQ = 24, 28, 32, 36


def _params(*sem):
    return pltpu.CompilerParams(dimension_semantics=sem, vmem_limit_bytes=VMEM_LIMIT)


def _rms(p, g):
    ms = jnp.mean(p * p, axis=-1, keepdims=True)
    return p * lax.rsqrt(ms + EPS) * g


def _nt_dot(a, b):
    return lax.dot_general(a, b, (((1,), (1,)), ((), ())), preferred_element_type=F32)


IN_TN = 1024


def _inproj_body(x_ref, g_ref, w_ref, gq_ref, cf_ref, sa_ref, sb_ref, o_ref, h_ref):
    j = pl.program_id(1)

    @pl.when(j == 0)
    def _():
        h_ref[...] = _rms(x_ref[...], g_ref[...]).astype(BF16)

    acc = jnp.dot(h_ref[...], w_ref[...], preferred_element_type=F32)
    ngrp = IN_TN // HEAD_DIM

    def grp(g):
        return acc[:, g * HEAD_DIM:(g + 1) * HEAD_DIM]

    def put(g, val):
        o_ref[:, g * HEAD_DIM:(g + 1) * HEAD_DIM] = val.astype(BF16)

    def rope(p):
        return (p * cf_ref[...] + pltpu.roll(p, HEAD_DIM - ROT_HALF, 1) * sa_ref[...]
                + pltpu.roll(p, ROT_HALF, 1) * sb_ref[...])

    for jj in (0, 1):
        @pl.when(j == jj)
        def _(jj=jj):
            gain = gq_ref[jj:jj + 1, :]
            for g in range(ngrp):
                put(g, rope(_rms(grp(g), gain)))

    @pl.when(j == 2)
    def _():
        o_ref[...] = acc.astype(BF16)

    @pl.when(j == 3)
    def _():
        for g in range(ngrp):
            row = 2 if g < ngrp // 2 else 3
            put(g, _rms(grp(g), gq_ref[row:row + 1, :]))

    @pl.when(j == 4)
    def _():
        for g in range(ngrp):
            if g < ngrp // 2:
                put(g, grp(g))
            else:
                put(g, _rms(grp(g), gq_ref[4:5, :]))


def _in_proj(x2, g, w_all, layer, gq, cf, sa, sb, seq, tm=1024):
    T, D = x2.shape
    nseq = seq // tm
    return pl.pallas_call(
        _inproj_body,
        grid=(T // tm, IN_WIDTH // IN_TN),
        in_specs=[
            pl.BlockSpec((tm, D), lambda i, j: (i, 0)),
            pl.BlockSpec((1, D), lambda i, j: (0, 0)),
            pl.BlockSpec((None, D, IN_TN), lambda i, j: (layer, 0, j)),
            pl.BlockSpec((6, HEAD_DIM), lambda i, j: (0, 0)),
            pl.BlockSpec((tm, HEAD_DIM), lambda i, j: (i % nseq, 0)),
            pl.BlockSpec((tm, HEAD_DIM), lambda i, j: (i % nseq, 0)),
            pl.BlockSpec((tm, HEAD_DIM), lambda i, j: (i % nseq, 0)),
        ],
        out_specs=pl.BlockSpec((tm, IN_TN), lambda i, j: (i, j)),
        out_shape=jax.ShapeDtypeStruct((T, IN_WIDTH), BF16),
        scratch_shapes=[pltpu.VMEM((tm, D), BF16)],
        compiler_params=_params("parallel", "arbitrary"),
        name="in_proj",
    )(x2, g, w_all, gq, cf, sa, sb)


DIFF_SUBQ = 256


def _diff_body(lq_ref, g_ref, q_ref, k_ref, v_ref, o_ref, *, lam_init):
    lq = lq_ref[...]
    lam = (jnp.exp(jnp.sum(lq[0:1] * lq[1:2], axis=-1, keepdims=True))
           - jnp.exp(jnp.sum(lq[2:3] * lq[3:4], axis=-1, keepdims=True)) + lam_init)

    def attend(rows, c):
        q = q_ref[rows, c * HEAD_DIM:(c + 1) * HEAD_DIM]
        k = k_ref[:, c * HEAD_DIM:(c + 1) * HEAD_DIM]
        s = _nt_dot(q, k)
        e = jnp.exp2((s - jnp.max(s, axis=-1, keepdims=True)) * (ATTN_SCALE * LOG2E))
        l = jnp.sum(e, axis=-1, keepdims=True)
        return jnp.dot(e.astype(BF16), v_ref[...], preferred_element_type=F32), l

    for r0 in range(0, q_ref.shape[0], DIFF_SUBQ):
        rows = slice(r0, r0 + DIFF_SUBQ)
        o1, l1 = attend(rows, 0)
        o2, l2 = attend(rows, 1)
        o = o1 * (1.0 / l1) - o2 * (lam / l2)
        o_ref[rows, :] = (_rms(o, g_ref[...]) * (1.0 - lam_init)).astype(o_ref.dtype)


def _diff_attention(proj3, lq, sub_g, lam_init, tq=1024):
    B, S, _ = proj3.shape
    return pl.pallas_call(
        functools.partial(_diff_body, lam_init=lam_init),
        grid=(B, N_DIFF_HEADS, S // tq),
        in_specs=[
            pl.BlockSpec((4, HEAD_DIM), lambda b, h, i: (0, 0)),
            pl.BlockSpec((1, DIFF_V_DIM), lambda b, h, i: (0, 0)),
            pl.BlockSpec((None, tq, DIFF_V_DIM), lambda b, h, i: (b, i, COL_DQ // 2 + h)),
            pl.BlockSpec((None, S, DIFF_V_DIM), lambda b, h, i: (b, 0, COL_DK // 2 + h)),
            pl.BlockSpec((None, S, DIFF_V_DIM), lambda b, h, i: (b, 0, COL_DV // 2 + h)),
        ],
        out_specs=pl.BlockSpec((None, tq, DIFF_V_DIM), lambda b, h, i: (b, i, h)),
        out_shape=jax.ShapeDtypeStruct((B, S, DIFF_WIDTH), BF16),
        compiler_params=_params("parallel", "parallel", "arbitrary"),
        name="diff_attn",
    )(lq, sub_g, proj3, proj3, proj3)


NA_QROWS = 8
NA_KROWS = 16
NA_TQ = NA_QROWS * GRID_W
NA_TK = NA_KROWS * GRID_W


def _na_table_body(b_ref, o_ref, *, rows):
    h = pl.program_id(0)
    variant = pl.program_id(1)
    qc = lax.broadcasted_iota(jnp.int32, (GRID_W, GRID_W), 0)
    kc = lax.broadcasted_iota(jnp.int32, (GRID_W, GRID_W), 1)
    cs = jnp.clip(qc - WIN_W // 2, 0, GRID_W - WIN_W)
    col_ok = (kc >= cs) & (kc < cs + WIN_W)
    dcol = kc - qc + (WIN_W - 1)
    masked = jnp.full((GRID_W, GRID_W), NEG_BIG, F32)
    col_tab = []
    for dr in range(2 * MAX_WIN_H - 1):
        t = masked
        for dc in range(2 * WIN_W - 1):
            t = jnp.where(dcol == dc, b_ref[h, dr, dc] * LOG2E, t)
        col_tab.append(jnp.where(col_ok, t, NEG_BIG))

    blocks = ((0, 0), (NA_QROWS, NA_QROWS - MAX_WIN_H // 2), (rows - NA_QROWS, rows - NA_KROWS))
    for vi, (r0, kb) in enumerate(blocks):
        @pl.when(variant == vi)
        def _(r0=r0, kb=kb):
            for qr in range(NA_QROWS):
                r = r0 + qr
                rs = min(max(r - MAX_WIN_H // 2, 0), rows - MAX_WIN_H)
                for kp in range(NA_KROWS // 2):
                    halves = []
                    for kr in (2 * kp, 2 * kp + 1):
                        ka = kb + kr
                        inside = rs <= ka < rs + MAX_WIN_H
                        halves.append(col_tab[ka - r + MAX_WIN_H - 1] if inside else masked)
                    o_ref[qr * GRID_W:(qr + 1) * GRID_W, kp * LANES:(kp + 1) * LANES] = (
                        jnp.concatenate(halves, axis=1))


def _na_bias_table(bias, rows):
    nh = bias.shape[0]
    return pl.pallas_call(
        functools.partial(_na_table_body, rows=rows),
        grid=(nh, 3),
        in_specs=[pl.BlockSpec(memory_space=pltpu.SMEM)],
        out_specs=pl.BlockSpec((None, None, NA_TQ, NA_TK), lambda h, v: (h, v, 0, 0)),
        out_shape=jax.ShapeDtypeStruct((nh, 3, NA_TQ, NA_TK), F32),
        compiler_params=_params("parallel", "arbitrary"),
        name="na_bias_table",
    )(bias)


def _na_body(q_ref, k_ref, v_ref, b_ref, o_ref, *, rows):
    i = pl.program_id(2)
    kb = jnp.clip(i * NA_QROWS - MAX_WIN_H // 2, 0, rows - NA_KROWS)
    start = pl.multiple_of(kb * GRID_W, GRID_W)
    k = k_ref[pl.ds(start, NA_TK), :]
    v = v_ref[pl.ds(start, NA_TK), :]
    t = _nt_dot(q_ref[...], k) * (ATTN_SCALE * LOG2E) + b_ref[...]
    e = jnp.exp2(t - jnp.max(t, axis=-1, keepdims=True))
    p = e * (1.0 / jnp.sum(e, axis=-1, keepdims=True))
    o_ref[...] = jnp.dot(p.astype(BF16), v, preferred_element_type=F32).astype(o_ref.dtype)


def _na_attention(proj3, bias_tab):
    B, S, _ = proj3.shape
    rows = S // GRID_W
    nblk = rows // NA_QROWS

    def bias_map(b, h, i):
        return (h, jnp.where(i == 0, 0, jnp.where(i == nblk - 1, 2, 1)), 0, 0)

    return pl.pallas_call(
        functools.partial(_na_body, rows=rows),
        grid=(B, N_NA_HEADS, nblk),
        in_specs=[
            pl.BlockSpec((None, NA_TQ, HEAD_DIM), lambda b, h, i: (b, i, COL_NQ + h)),
            pl.BlockSpec((None, S, HEAD_DIM), lambda b, h, i: (b, 0, COL_NK + h)),
            pl.BlockSpec((None, S, HEAD_DIM), lambda b, h, i: (b, 0, COL_NV + h)),
            pl.BlockSpec((None, None, NA_TQ, NA_TK), bias_map),
        ],
        out_specs=pl.BlockSpec((None, NA_TQ, HEAD_DIM), lambda b, h, i: (b, i, h)),
        out_shape=jax.ShapeDtypeStruct((B, S, NA_WIDTH), BF16),
        compiler_params=_params("parallel", "parallel", "arbitrary"),
        name="na_attn",
    )(proj3, proj3, proj3, bias_tab)


def _memkv_body(m_ref, g_ref, w_ref, gk_ref, k_ref, v_ref):
    hm = _rms(m_ref[...], g_ref[...]).astype(BF16)
    kv = jnp.dot(hm, w_ref[...], preferred_element_type=F32)
    for h in range(N_MEM_HEADS):
        sl = slice(h * HEAD_DIM, (h + 1) * HEAD_DIM)
        k_ref[:, sl] = _rms(kv[:, sl], gk_ref[...]).astype(BF16)
    v_ref[...] = kv[:, MEM_WIDTH:].astype(BF16)


def _mem_kv(mem, g, w_bf, gk):
    B, M, D = mem.shape
    return pl.pallas_call(
        _memkv_body,
        grid=(B,),
        in_specs=[
            pl.BlockSpec((None, M, D), lambda b: (b, 0, 0)),
            pl.BlockSpec((1, D), lambda b: (0, 0)),
            pl.BlockSpec((D, 2 * MEM_WIDTH), lambda b: (0, 0)),
            pl.BlockSpec((1, HEAD_DIM), lambda b: (0, 0)),
        ],
        out_specs=[pl.BlockSpec((None, M, MEM_WIDTH), lambda b: (b, 0, 0))] * 2,
        out_shape=[jax.ShapeDtypeStruct((B, M, MEM_WIDTH), BF16)] * 2,
        compiler_params=_params("parallel"),
        name="mem_kv",
    )(mem, g, w_bf, gk)


def _memattn_body(q_ref, k_ref, v_ref, o_ref):
    for h in range(N_MEM_HEADS):
        sl = slice(h * HEAD_DIM, (h + 1) * HEAD_DIM)
        s = _nt_dot(q_ref[:, sl], k_ref[:, sl])
        e = jnp.exp2((s - jnp.max(s, axis=-1, keepdims=True)) * (ATTN_SCALE * LOG2E))
        p = e * (1.0 / jnp.sum(e, axis=-1, keepdims=True))
        o_ref[:, sl] = jnp.dot(p.astype(BF16), v_ref[:, sl], preferred_element_type=F32).astype(o_ref.dtype)


def _mem_attention(proj3, mk, mv, tq=512):
    B, S, _ = proj3.shape
    M = mk.shape[1]
    return pl.pallas_call(
        _memattn_body,
        grid=(B, S // tq),
        in_specs=[
            pl.BlockSpec((None, tq, MEM_WIDTH), lambda b, i: (b, i, COL_MQ // N_MEM_HEADS)),
            pl.BlockSpec((None, M, MEM_WIDTH), lambda b, i: (b, 0, 0)),
            pl.BlockSpec((None, M, MEM_WIDTH), lambda b, i: (b, 0, 0)),
        ],
        out_specs=pl.BlockSpec((None, tq, MEM_WIDTH), lambda b, i: (b, i, 0)),
        out_shape=jax.ShapeDtypeStruct((B, S, MEM_WIDTH), BF16),
        compiler_params=_params("parallel", "arbitrary"),
        name="mem_attn",
    )(proj3, mk, mv)


def _outproj_body(x_ref, od_ref, on_ref, om_ref, w_ref, g_ref, xn_ref, h_ref, ht_ref):
    mix = jnp.concatenate([od_ref[...], on_ref[...], om_ref[...]], axis=-1)
    xn = x_ref[...] + jnp.dot(mix, w_ref[...], preferred_element_type=F32)
    xn_ref[...] = xn
    h = _rms(xn, g_ref[...])
    h_ref[...] = h.astype(BF16)
    ht_ref[...] = h.T.astype(BF16)


def _out_proj(x2, od, on, om, w_bf, g, tm=512):
    T, D = x2.shape
    return pl.pallas_call(
        _outproj_body,
        grid=(T // tm,),
        in_specs=[
            pl.BlockSpec((tm, D), lambda i: (i, 0)),
            pl.BlockSpec((tm, DIFF_WIDTH), lambda i: (i, 0)),
            pl.BlockSpec((tm, NA_WIDTH), lambda i: (i, 0)),
            pl.BlockSpec((tm, MEM_WIDTH), lambda i: (i, 0)),
            pl.BlockSpec((D, D), lambda i: (0, 0)),
            pl.BlockSpec((1, D), lambda i: (0, 0)),
        ],
        out_specs=[pl.BlockSpec((tm, D), lambda i: (i, 0))] * 2 + [pl.BlockSpec((D, tm), lambda i: (0, i))],
        out_shape=[jax.ShapeDtypeStruct((T, D), F32), jax.ShapeDtypeStruct((T, D), BF16),
                   jax.ShapeDtypeStruct((D, T), BF16)],
        compiler_params=_params("parallel"),
        name="out_proj",
    )(x2, od, on, om, w_bf, g)


KH = N_KEYS * PEER_HEADS


def _sort_pairs(n):
    pairs = []

    def merge(lo, hi, r):
        step = r * 2
        if step < hi - lo:
            merge(lo, hi, step)
            merge(lo + r, hi, step)
            pairs.extend((i, i + r) for i in range(lo + r, hi - r, step))
        else:
            pairs.append((lo, lo + r))

    def sort(lo, hi):
        if hi > lo:
            mid = lo + (hi - lo) // 2
            sort(lo, mid)
            sort(mid + 1, hi)
            merge(lo, hi, 1)

    sort(0, n - 1)
    return pairs


_SORT16 = _sort_pairs(PEER_TOPK)


def _exchange(v, i, j):
    v[i], v[j] = jnp.maximum(v[i], v[j]), jnp.minimum(v[i], v[j])


def _sorted_desc(v):
    v = list(v)
    for i, j in _SORT16:
        _exchange(v, i, j)
    return v


def _merge_top(a, b):
    k = PEER_TOPK
    c = list(a)
    for i in range(len(b)):
        c[k - 1 - i] = jnp.maximum(a[k - 1 - i], b[i])
    d = k // 2
    while d >= 1:
        for i in range(k):
            if i & d == 0:
                _exchange(c, i, i + d)
        d //= 2
    return c


def _top16_of_keys(s_ref):
    best = None
    for g in range(N_KEYS // PEER_TOPK):
        grp = _sorted_desc([s_ref[(g * PEER_TOPK + r) * PEER_HEADS:(g * PEER_TOPK + r + 1) * PEER_HEADS, :]
                            for r in range(PEER_TOPK)])
        best = grp if best is None else _merge_top(best, grp)
    return best


def _router_body(h_ref, w_ref, kexp_ref, s1_ref, s2_ref, b_ref, tau_ref, mlz_ref, se_ref):
    tr = h_ref.shape[0]
    q = jnp.dot(h_ref[...], w_ref[...], preferred_element_type=F32).astype(BF16)
    for c in range(2):
        s = _nt_dot(kexp_ref[c], q[:, c * KH:(c + 1) * KH]) * LOG2E
        if c == 0:
            s1_ref[...] = s
        for lc in range(tr // LANES):
            se_ref[c, lc] = s[:, lc * LANES:(lc + 1) * LANES]

    k = PEER_TOPK
    for lc in range(tr // LANES):
        lanes = slice(lc * LANES, (lc + 1) * LANES)
        for h in range(PEER_HEADS):
            s2_ref[h, :, lanes] = se_ref[1, lc, pl.ds(h, N_KEYS, stride=PEER_HEADS), :]
        a = _top16_of_keys(se_ref.at[0, lc])
        b = _top16_of_keys(se_ref.at[1, lc])
        best = [a[0] + b[r] for r in range(k)]
        lists = [[a[r1] + b[r2] for r2 in range(k // (r1 + 1))] for r1 in range(1, k // 2)]
        lists.append([a[r1] + b[0] for r1 in range(k // 2, k)])
        for lst in lists:
            best = _merge_top(best, lst)
        z = jnp.ones_like(best[0])
        for v in best[1:]:
            z = z + jnp.exp2(v - best[0])
        for r in range(k):
            b_ref[r, :, lanes] = b[r]
        tau_ref[:, lanes] = best[k - 1]
        mlz_ref[:, lanes] = best[0] + jnp.log2(z) + 1.0


def _router(h2, w_bf, kexp_bf, tr=256):
    T, D = h2.shape
    qw = w_bf.shape[1]
    return pl.pallas_call(
        _router_body,
        grid=(T // tr,),
        in_specs=[
            pl.BlockSpec((tr, D), lambda i: (i, 0)),
            pl.BlockSpec((D, qw), lambda i: (0, 0)),
            pl.BlockSpec((2, KH, KH), lambda i: (0, 0, 0)),
        ],
        out_specs=[
            pl.BlockSpec((KH, tr), lambda i: (0, i)),
            pl.BlockSpec((PEER_HEADS, N_KEYS, tr), lambda i: (0, 0, i)),
            pl.BlockSpec((PEER_TOPK, PEER_HEADS, tr), lambda i: (0, 0, i)),
            pl.BlockSpec((PEER_HEADS, tr), lambda i: (0, i)),
            pl.BlockSpec((PEER_HEADS, tr), lambda i: (0, i)),
        ],
        out_shape=[
            jax.ShapeDtypeStruct((KH, T), F32),
            jax.ShapeDtypeStruct((PEER_HEADS, N_KEYS, T), F32),
            jax.ShapeDtypeStruct((PEER_TOPK, PEER_HEADS, T), F32),
            jax.ShapeDtypeStruct((PEER_HEADS, T), F32),
            jax.ShapeDtypeStruct((PEER_HEADS, T), F32),
        ],
        scratch_shapes=[pltpu.VMEM((2, tr // LANES, KH, LANES), F32)],
        compiler_params=_params("parallel"),
        name="peer_router",
    )(h2, w_bf, kexp_bf)


def _router_weights(w_pq, keys):
    D = w_pq.shape[0]
    wp = w_pq.reshape(D, PEER_HEADS, 2, N_KEYS).transpose(0, 2, 1, 3).reshape(D, 2 * KH)
    kt = keys.transpose(1, 2, 0, 3)
    same_head = jnp.eye(PEER_HEADS, dtype=bool)[None, None, :, :, None]
    kexp = jnp.where(same_head, kt[:, :, :, None, :], 0.0).reshape(2, KH, KH)
    return wp.astype(BF16), kexp.astype(BF16)


PEER_TC = 128
PEER_TR = 64
PEER_SUB = 256


def _peer_body(ht_ref, u_ref, vt_ref, s1_ref, s2_ref, b_ref, tau_ref, mlz_ref, x_ref, o_ref,
               acc_ref, thr_ref, off_ref, g_ref, *a_refs):
    j = pl.program_id(1)
    nsub = len(a_refs)
    tt = ht_ref.shape[1]
    n_i1 = s1_ref.shape[0]
    keys_per_sub = PEER_SUB // N_KEYS
    sub8 = PEER_TR // 8

    @pl.when(j == 0)
    def _():
        acc_ref[...] = jnp.zeros_like(acc_ref)

    for il in range(n_i1):
        for tc in range(tt // PEER_TC):
            lanes = slice(tc * PEER_TC, (tc + 1) * PEER_TC)
            s1 = s1_ref[il, :, lanes]
            tau = tau_ref[:, lanes]
            thr = jnp.full_like(s1, jnp.inf)
            for r in range(PEER_TOPK):
                b = b_ref[r, :, lanes]
                thr = jnp.where(s1 + b >= tau, b, thr)
            off = s1 - mlz_ref[:, lanes]
            for h in range(PEER_HEADS):
                row = il * PEER_HEADS + h
                thr_ref[row, :, lanes] = jnp.broadcast_to(thr[h:h + 1, :], (8, PEER_TC))
                off_ref[row, :, lanes] = jnp.broadcast_to(off[h:h + 1, :], (8, PEER_TC))

    def scores(k):
        a_refs[k][...] = jnp.dot(u_ref[k * PEER_SUB:(k + 1) * PEER_SUB, :], ht_ref[...],
                                 preferred_element_type=F32)

    def gate(k):
        a_ref = a_refs[k]
        for il in range(keys_per_sub):
            i1 = k * keys_per_sub + il
            for tc in range(tt // PEER_TC):
                lanes = slice(tc * PEER_TC, (tc + 1) * PEER_TC)
                for r0 in range(0, N_KEYS, PEER_TR):
                    w = None
                    for h in range(PEER_HEADS):
                        row = i1 * PEER_HEADS + h
                        s2 = s2_ref[h, r0:r0 + PEER_TR, lanes].reshape(sub8, 8, PEER_TC)
                        wh = jnp.where(s2 >= thr_ref[row, :, lanes][None],
                                       jnp.exp2(s2 + off_ref[row, :, lanes][None]), 0.0)
                        w = wh if w is None else w + wh
                    a = a_ref[il * N_KEYS + r0:il * N_KEYS + r0 + PEER_TR, lanes].reshape(sub8, 8, PEER_TC)
                    g = w * (a * (1.0 + lax.erf(a * SQRT_HALF)))
                    g_ref[i1 * N_KEYS + r0:i1 * N_KEYS + r0 + PEER_TR, lanes] = (
                        g.reshape(PEER_TR, PEER_TC).astype(BF16))

    for k in range(nsub):
        scores(k)
        gate(k)
    acc_ref[...] += jnp.dot(vt_ref[...], g_ref[...], preferred_element_type=F32)

    @pl.when(j == pl.num_programs(1) - 1)
    def _():
        o_ref[...] = x_ref[...] + acc_ref[...].T


def _peer_dense(xn, ht, u_all, vt_all, layer, s1, s2, b2, tau, mlz, tt=512, te=1024):
    D, T = ht.shape
    E = u_all.shape[1]
    n_i1 = te // N_KEYS
    nsub = te // PEER_SUB
    return pl.pallas_call(
        _peer_body,
        grid=(T // tt, E // te),
        in_specs=[
            pl.BlockSpec((D, tt), lambda i, j: (0, i)),
            pl.BlockSpec((None, te, D), lambda i, j: (layer, j, 0)),
            pl.BlockSpec((None, D, te), lambda i, j: (layer, 0, j)),
            pl.BlockSpec((n_i1, PEER_HEADS, tt), lambda i, j: (j, 0, i)),
            pl.BlockSpec((PEER_HEADS, N_KEYS, tt), lambda i, j: (0, 0, i)),
            pl.BlockSpec((PEER_TOPK, PEER_HEADS, tt), lambda i, j: (0, 0, i)),
            pl.BlockSpec((PEER_HEADS, tt), lambda i, j: (0, i)),
            pl.BlockSpec((PEER_HEADS, tt), lambda i, j: (0, i)),
            pl.BlockSpec((tt, D), lambda i, j: (i, 0)),
        ],
        out_specs=pl.BlockSpec((tt, D), lambda i, j: (i, 0)),
        out_shape=jax.ShapeDtypeStruct((T, D), F32),
        scratch_shapes=([pltpu.VMEM((D, tt), F32)] + [pltpu.VMEM((n_i1 * PEER_HEADS, 8, tt), F32)] * 2
                        + [pltpu.VMEM((te, tt), BF16)] + [pltpu.VMEM((PEER_SUB, tt), F32)] * nsub),
        compiler_params=_params("parallel", "arbitrary"),
        name="peer_dense",
    )(ht, u_all, vt_all, s1.reshape(N_KEYS, PEER_HEADS, T), s2, b2, tau, mlz, xn)


def _vt_body(v_ref, o_ref):
    o_ref[...] = v_ref[...].T.astype(BF16)


def _value_table_t(peer_v, te=512):
    depth, E, D = peer_v.shape
    return pl.pallas_call(
        _vt_body,
        grid=(depth, E // te),
        in_specs=[pl.BlockSpec((None, te, D), lambda l, j: (l, j, 0))],
        out_specs=pl.BlockSpec((None, D, te), lambda l, j: (l, 0, j)),
        out_shape=jax.ShapeDtypeStruct((depth, D, E), BF16),
        compiler_params=_params("parallel", "parallel"),
        name="value_table_t",
    )(peer_v)


def _rope_tables(seq):
    inv = ROPE_THETA ** (-jnp.arange(0, ROT_DIM, 2, dtype=jnp.float32) / ROT_DIM)
    ang = jnp.arange(seq, dtype=jnp.float32)[:, None] * inv[None, :]
    cos, sin = jnp.cos(ang), jnp.sin(ang)
    rest = HEAD_DIM - ROT_DIM
    zeros_h = jnp.zeros((seq, ROT_HALF), F32)
    cf = jnp.concatenate([cos, cos, jnp.ones((seq, rest), F32)], axis=-1)
    sa = jnp.concatenate([-sin, zeros_h, jnp.zeros((seq, rest), F32)], axis=-1)
    sb = jnp.concatenate([zeros_h, sin, jnp.zeros((seq, rest), F32)], axis=-1)
    return cf, sa, sb


def kernel(x, mem, attn_norm, w_in, qk_gain, lambda_qk, subln_gain, na_bias, mem_norm, w_mem_kv,
           w_out, ffn_norm, w_pq, peer_keys, peer_u, peer_v):
    B, S, D = x.shape
    depth = w_in.shape[0]
    T = B * S
    cf, sa, sb = _rope_tables(S)
    x2 = x.reshape(T, D)
    w_in_all = w_in.astype(BF16)
    u_all = peer_u.astype(BF16)
    vt_all = _value_table_t(peer_v)
    for l in range(depth):
        lam_init = 0.8 - 0.6 * math.exp(-0.3 * l)
        proj = _in_proj(x2, attn_norm[l][None], w_in_all, l, qk_gain[l], cf, sa, sb, S)
        proj3 = proj.reshape(B, S, IN_WIDTH)
        o_diff = _diff_attention(proj3, lambda_qk[l], subln_gain[l][None], lam_init)
        o_na = _na_attention(proj3, _na_bias_table(na_bias[l], S // GRID_W))
        mk, mv = _mem_kv(mem, mem_norm[l][None], w_mem_kv[l].astype(BF16), qk_gain[l][5:6])
        o_mem = _mem_attention(proj3, mk, mv)
        xn, h2, h2t = _out_proj(x2, o_diff.reshape(T, DIFF_WIDTH), o_na.reshape(T, NA_WIDTH),
                                o_mem.reshape(T, MEM_WIDTH), w_out[l].astype(BF16), ffn_norm[l][None])
        s1, s2, b2, tau, mlz = _router(h2, *_router_weights(w_pq[l], peer_keys[l]))
        x2 = _peer_dense(xn, h2t, u_all, vt_all, l, s1, s2, b2, tau, mlz)
    return x2.reshape(B, S, D)
```

```python
import functools
import math

import jax
import jax.numpy as jnp
from jax import lax
from jax.experimental import pallas as pl
from jax.experimental.pallas import tpu as pltpu

F32 = jnp.float32
BF16 = jnp.bfloat16

HEAD_DIM = 128
N_DIFF_HEADS = 4
N_NA_HEADS = 4
N_MEM_HEADS = 4
DIFF_V_DIM = 2 * HEAD_DIM
DIFF_WIDTH = N_DIFF_HEADS * DIFF_V_DIM
NA_WIDTH = N_NA_HEADS * HEAD_DIM
MEM_WIDTH = N_MEM_HEADS * HEAD_DIM
IN_WIDTH = 3 * DIFF_WIDTH + 3 * NA_WIDTH + MEM_WIDTH
ROPE_THETA = 500000.0
ROT_DIM = HEAD_DIM // 4
ROT_HALF = ROT_DIM // 2
GRID_W = 64
MAX_WIN_H = 8
WIN_W = 16
PEER_HEADS = 8
N_KEYS = 128
PEER_TOPK = 16
EPS = 1e-6
ATTN_SCALE = HEAD_DIM ** -0.5
NEG_BIG = -1e30
SQRT_HALF = math.sqrt(0.5)
LOG2E = math.log2(math.e)

LANES = 128
SUBLANES = 8
VMEM_LIMIT = 56 * 1024 * 1024

COL_DQ, COL_DK, COL_DV = 0, 8, 16
COL_NQ, COL_NK, COL_NV, COL_MQ = 24, 28, 32, 36


def _params(*sem):
    return pltpu.CompilerParams(dimension_semantics=sem, vmem_limit_bytes=VMEM_LIMIT)


def _rms(p, g):
    ms = jnp.mean(p * p, axis=-1, keepdims=True)
    return p * lax.rsqrt(ms + EPS) * g


def _nt_dot(a, b):
    return lax.dot_general(a, b, (((1,), (1,)), ((), ())), preferred_element_type=F32)


IN_TN = 1024


def _inproj_body(x_ref, g_ref, w_ref, gq_ref, cf_ref, sa_ref, sb_ref, o_ref, h_ref):
    j = pl.program_id(1)

    @pl.when(j == 0)
    def _():
        h_ref[...] = _rms(x_ref[...], g_ref[...]).astype(BF16)

    acc = jnp.dot(h_ref[...], w_ref[...], preferred_element_type=F32)
    ngrp = IN_TN // HEAD_DIM

    def grp(g):
        return acc[:, g * HEAD_DIM:(g + 1) * HEAD_DIM]

    def put(g, val):
        o_ref[:, g * HEAD_DIM:(g + 1) * HEAD_DIM] = val.astype(BF16)

    def rope(p):
        return (p * cf_ref[...] + pltpu.roll(p, HEAD_DIM - ROT_HALF, 1) * sa_ref[...]
                + pltpu.roll(p, ROT_HALF, 1) * sb_ref[...])

    for jj in (0, 1):
        @pl.when(j == jj)
        def _(jj=jj):
            gain = gq_ref[jj:jj + 1, :]
            for g in range(ngrp):
                put(g, rope(_rms(grp(g), gain)))

    @pl.when(j == 2)
    def _():
        o_ref[...] = acc.astype(BF16)

    @pl.when(j == 3)
    def _():
        for g in range(ngrp):
            row = 2 if g < ngrp // 2 else 3
            put(g, _rms(grp(g), gq_ref[row:row + 1, :]))

    @pl.when(j == 4)
    def _():
        for g in range(ngrp):
            if g < ngrp // 2:
                put(g, grp(g))
            else:
                put(g, _rms(grp(g), gq_ref[4:5, :]))


def _in_proj(x2, g, w_all, layer, gq, cf, sa, sb, seq, tm=1024):
    T, D = x2.shape
    nseq = seq // tm
    return pl.pallas_call(
        _inproj_body,
        grid=(T // tm, IN_WIDTH // IN_TN),
        in_specs=[
            pl.BlockSpec((tm, D), lambda i, j: (i, 0)),
            pl.BlockSpec((1, D), lambda i, j: (0, 0)),
            pl.BlockSpec((None, D, IN_TN), lambda i, j: (layer, 0, j)),
            pl.BlockSpec((6, HEAD_DIM), lambda i, j: (0, 0)),
            pl.BlockSpec((tm, HEAD_DIM), lambda i, j: (i % nseq, 0)),
            pl.BlockSpec((tm, HEAD_DIM), lambda i, j: (i % nseq, 0)),
            pl.BlockSpec((tm, HEAD_DIM), lambda i, j: (i % nseq, 0)),
        ],
        out_specs=pl.BlockSpec((tm, IN_TN), lambda i, j: (i, j)),
        out_shape=jax.ShapeDtypeStruct((T, IN_WIDTH), BF16),
        scratch_shapes=[pltpu.VMEM((tm, D), BF16)],
        compiler_params=_params("parallel", "arbitrary"),
        name="in_proj",
    )(x2, g, w_all, gq, cf, sa, sb)


DIFF_SUBQ = 256


def _diff_body(lq_ref, g_ref, q_ref, k_ref, v_ref, o_ref, *, lam_init):
    lq = lq_ref[...]
    lam = (jnp.exp(jnp.sum(lq[0:1] * lq[1:2], axis=-1, keepdims=True))
           - jnp.exp(jnp.sum(lq[2:3] * lq[3:4], axis=-1, keepdims=True)) + lam_init)

    def attend(rows, c):
        q = q_ref[rows, c * HEAD_DIM:(c + 1) * HEAD_DIM]
        k = k_ref[:, c * HEAD_DIM:(c + 1) * HEAD_DIM]
        s = _nt_dot(q, k)
        e = jnp.exp2((s - jnp.max(s, axis=-1, keepdims=True)) * (ATTN_SCALE * LOG2E))
        l = jnp.sum(e, axis=-1, keepdims=True)
        return jnp.dot(e.astype(BF16), v_ref[...], preferred_element_type=F32), l

    for r0 in range(0, q_ref.shape[0], DIFF_SUBQ):
        rows = slice(r0, r0 + DIFF_SUBQ)
        o1, l1 = attend(rows, 0)
        o2, l2 = attend(rows, 1)
        o = o1 * (1.0 / l1) - o2 * (lam / l2)
        o_ref[rows, :] = (_rms(o, g_ref[...]) * (1.0 - lam_init)).astype(o_ref.dtype)


def _diff_attention(proj3, lq, sub_g, lam_init, tq=1024):
    B, S, _ = proj3.shape
    return pl.pallas_call(
        functools.partial(_diff_body, lam_init=lam_init),
        grid=(B, N_DIFF_HEADS, S // tq),
        in_specs=[
            pl.BlockSpec((4, HEAD_DIM), lambda b, h, i: (0, 0)),
            pl.BlockSpec((1, DIFF_V_DIM), lambda b, h, i: (0, 0)),
            pl.BlockSpec((None, tq, DIFF_V_DIM), lambda b, h, i: (b, i, COL_DQ // 2 + h)),
            pl.BlockSpec((None, S, DIFF_V_DIM), lambda b, h, i: (b, 0, COL_DK // 2 + h)),
            pl.BlockSpec((None, S, DIFF_V_DIM), lambda b, h, i: (b, 0, COL_DV // 2 + h)),
        ],
        out_specs=pl.BlockSpec((None, tq, DIFF_V_DIM), lambda b, h, i: (b, i, h)),
        out_shape=jax.ShapeDtypeStruct((B, S, DIFF_WIDTH), BF16),
        compiler_params=_params("parallel", "parallel", "arbitrary"),
        name="diff_attn",
    )(lq, sub_g, proj3, proj3, proj3)


NA_QROWS = 8
NA_KROWS = 16
NA_TQ = NA_QROWS * GRID_W
NA_TK = NA_KROWS * GRID_W


def _na_table_body(b_ref, o_ref, *, rows):
    h = pl.program_id(0)
    variant = pl.program_id(1)
    qc = lax.broadcasted_iota(jnp.int32, (GRID_W, GRID_W), 0)
    kc = lax.broadcasted_iota(jnp.int32, (GRID_W, GRID_W), 1)
    cs = jnp.clip(qc - WIN_W // 2, 0, GRID_W - WIN_W)
    col_ok = (kc >= cs) & (kc < cs + WIN_W)
    dcol = kc - qc + (WIN_W - 1)
    masked = jnp.full((GRID_W, GRID_W), NEG_BIG, F32)
    col_tab = []
    for dr in range(2 * MAX_WIN_H - 1):
        t = masked
        for dc in range(2 * WIN_W - 1):
            t = jnp.where(dcol == dc, b_ref[h, dr, dc] * LOG2E, t)
        col_tab.append(jnp.where(col_ok, t, NEG_BIG))

    blocks = ((0, 0), (NA_QROWS, NA_QROWS - MAX_WIN_H // 2), (rows - NA_QROWS, rows - NA_KROWS))
    for vi, (r0, kb) in enumerate(blocks):
        @pl.when(variant == vi)
        def _(r0=r0, kb=kb):
            for qr in range(NA_QROWS):
                r = r0 + qr
                rs = min(max(r - MAX_WIN_H // 2, 0), rows - MAX_WIN_H)
                for kp in range(NA_KROWS // 2):
                    halves = []
                    for kr in (2 * kp, 2 * kp + 1):
                        ka = kb + kr
                        inside = rs <= ka < rs + MAX_WIN_H
                        halves.append(col_tab[ka - r + MAX_WIN_H - 1] if inside else masked)
                    o_ref[qr * GRID_W:(qr + 1) * GRID_W, kp * LANES:(kp + 1) * LANES] = (
                        jnp.concatenate(halves, axis=1))


def _na_bias_table(bias, rows):
    nh = bias.shape[0]
    return pl.pallas_call(
        functools.partial(_na_table_body, rows=rows),
        grid=(nh, 3),
        in_specs=[pl.BlockSpec(memory_space=pltpu.SMEM)],
        out_specs=pl.BlockSpec((None, None, NA_TQ, NA_TK), lambda h, v: (h, v, 0, 0)),
        out_shape=jax.ShapeDtypeStruct((nh, 3, NA_TQ, NA_TK), F32),
        compiler_params=_params("parallel", "arbitrary"),
        name="na_bias_table",
    )(bias)


def _na_body(q_ref, k_ref, v_ref, b_ref, o_ref, *, rows):
    i = pl.program_id(2)
    kb = jnp.clip(i * NA_QROWS - MAX_WIN_H // 2, 0, rows - NA_KROWS)
    start = pl.multiple_of(kb * GRID_W, GRID_W)
    k = k_ref[pl.ds(start, NA_TK), :]
    v = v_ref[pl.ds(start, NA_TK), :]
    t = _nt_dot(q_ref[...], k) * (ATTN_SCALE * LOG2E) + b_ref[...]
    e = jnp.exp2(t - jnp.max(t, axis=-1, keepdims=True))
    p = e * (1.0 / jnp.sum(e, axis=-1, keepdims=True))
    o_ref[...] = jnp.dot(p.astype(BF16), v, preferred_element_type=F32).astype(o_ref.dtype)


def _na_attention(proj3, bias_tab):
    B, S, _ = proj3.shape
    rows = S // GRID_W
    nblk = rows // NA_QROWS

    def bias_map(b, h, i):
        return (h, jnp.where(i == 0, 0, jnp.where(i == nblk - 1, 2, 1)), 0, 0)

    return pl.pallas_call(
        functools.partial(_na_body, rows=rows),
        grid=(B, N_NA_HEADS, nblk),
        in_specs=[
            pl.BlockSpec((None, NA_TQ, HEAD_DIM), lambda b, h, i: (b, i, COL_NQ + h)),
            pl.BlockSpec((None, S, HEAD_DIM), lambda b, h, i: (b, 0, COL_NK + h)),
            pl.BlockSpec((None, S, HEAD_DIM), lambda b, h, i: (b, 0, COL_NV + h)),
            pl.BlockSpec((None, None, NA_TQ, NA_TK), bias_map),
        ],
        out_specs=pl.BlockSpec((None, NA_TQ, HEAD_DIM), lambda b, h, i: (b, i, h)),
        out_shape=jax.ShapeDtypeStruct((B, S, NA_WIDTH), BF16),
        compiler_params=_params("parallel", "parallel", "arbitrary"),
        name="na_attn",
    )(proj3, proj3, proj3, bias_tab)


def _memkv_body(m_ref, g_ref, w_ref, gk_ref, k_ref, v_ref):
    hm = _rms(m_ref[...], g_ref[...]).astype(BF16)
    kv = jnp.dot(hm, w_ref[...], preferred_element_type=F32)
    for h in range(N_MEM_HEADS):
        sl = slice(h * HEAD_DIM, (h + 1) * HEAD_DIM)
        k_ref[:, sl] = _rms(kv[:, sl], gk_ref[...]).astype(BF16)
    v_ref[...] = kv[:, MEM_WIDTH:].astype(BF16)


def _mem_kv(mem, g, w_bf, gk):
    B, M, D = mem.shape
    return pl.pallas_call(
        _memkv_body,
        grid=(B,),
        in_specs=[
            pl.BlockSpec((None, M, D), lambda b: (b, 0, 0)),
            pl.BlockSpec((1, D), lambda b: (0, 0)),
            pl.BlockSpec((D, 2 * MEM_WIDTH), lambda b: (0, 0)),
            pl.BlockSpec((1, HEAD_DIM), lambda b: (0, 0)),
        ],
        out_specs=[pl.BlockSpec((None, M, MEM_WIDTH), lambda b: (b, 0, 0))] * 2,
        out_shape=[jax.ShapeDtypeStruct((B, M, MEM_WIDTH), BF16)] * 2,
        compiler_params=_params("parallel"),
        name="mem_kv",
    )(mem, g, w_bf, gk)


def _memattn_body(q_ref, k_ref, v_ref, o_ref):
    for h in range(N_MEM_HEADS):
        sl = slice(h * HEAD_DIM, (h + 1) * HEAD_DIM)
        s = _nt_dot(q_ref[:, sl], k_ref[:, sl])
        e = jnp.exp2((s - jnp.max(s, axis=-1, keepdims=True)) * (ATTN_SCALE * LOG2E))
        p = e * (1.0 / jnp.sum(e, axis=-1, keepdims=True))
        o_ref[:, sl] = jnp.dot(p.astype(BF16), v_ref[:, sl], preferred_element_type=F32).astype(o_ref.dtype)


def _mem_attention(proj3, mk, mv, tq=512):
    B, S, _ = proj3.shape
    M = mk.shape[1]
    return pl.pallas_call(
        _memattn_body,
        grid=(B, S // tq),
        in_specs=[
            pl.BlockSpec((None, tq, MEM_WIDTH), lambda b, i: (b, i, COL_MQ // N_MEM_HEADS)),
            pl.BlockSpec((None, M, MEM_WIDTH), lambda b, i: (b, 0, 0)),
            pl.BlockSpec((None, M, MEM_WIDTH), lambda b, i: (b, 0, 0)),
        ],
        out_specs=pl.BlockSpec((None, tq, MEM_WIDTH), lambda b, i: (b, i, 0)),
        out_shape=jax.ShapeDtypeStruct((B, S, MEM_WIDTH), BF16),
        compiler_params=_params("parallel", "arbitrary"),
        name="mem_attn",
    )(proj3, mk, mv)


def _outproj_body(x_ref, od_ref, on_ref, om_ref, w_ref, g_ref, xn_ref, h_ref, ht_ref):
    mix = jnp.concatenate([od_ref[...], on_ref[...], om_ref[...]], axis=-1)
    xn = x_ref[...] + jnp.dot(mix, w_ref[...], preferred_element_type=F32)
    xn_ref[...] = xn
    h = _rms(xn, g_ref[...])
    h_ref[...] = h.astype(BF16)
    ht_ref[...] = h.T.astype(BF16)


def _out_proj(x2, od, on, om, w_bf, g, tm=512):
    T, D = x2.shape
    return pl.pallas_call(
        _outproj_body,
        grid=(T // tm,),
        in_specs=[
            pl.BlockSpec((tm, D), lambda i: (i, 0)),
            pl.BlockSpec((tm, DIFF_WIDTH), lambda i: (i, 0)),
            pl.BlockSpec((tm, NA_WIDTH), lambda i: (i, 0)),
            pl.BlockSpec((tm, MEM_WIDTH), lambda i: (i, 0)),
            pl.BlockSpec((D, D), lambda i: (0, 0)),
            pl.BlockSpec((1, D), lambda i: (0, 0)),
        ],
        out_specs=[pl.BlockSpec((tm, D), lambda i: (i, 0))] * 2 + [pl.BlockSpec((D, tm), lambda i: (0, i))],
        out_shape=[jax.ShapeDtypeStruct((T, D), F32), jax.ShapeDtypeStruct((T, D), BF16),
                   jax.ShapeDtypeStruct((D, T), BF16)],
        compiler_params=_params("parallel"),
        name="out_proj",
    )(x2, od, on, om, w_bf, g)


KH = N_KEYS * PEER_HEADS


def _sort_pairs(n):
    pairs = []

    def merge(lo, hi, r):
        step = r * 2
        if step < hi - lo:
            merge(lo, hi, step)
            merge(lo + r, hi, step)
            pairs.extend((i, i + r) for i in range(lo + r, hi - r, step))
        else:
            pairs.append((lo, lo + r))

    def sort(lo, hi):
        if hi > lo:
            mid = lo + (hi - lo) // 2
            sort(lo, mid)
            sort(mid + 1, hi)
            merge(lo, hi, 1)

    sort(0, n - 1)
    return pairs


_SORT16 = _sort_pairs(PEER_TOPK)


def _exchange(v, i, j):
    v[i], v[j] = jnp.maximum(v[i], v[j]), jnp.minimum(v[i], v[j])


def _sorted_desc(v):
    v = list(v)
    for i, j in _SORT16:
        _exchange(v, i, j)
    return v


def _merge_top(a, b):
    k = PEER_TOPK
    c = list(a)
    for i in range(len(b)):
        c[k - 1 - i] = jnp.maximum(a[k - 1 - i], b[i])
    d = k // 2
    while d >= 1:
        for i in range(k):
            if i & d == 0:
                _exchange(c, i, i + d)
        d //= 2
    return c


def _top16_of_keys(s_ref):
    best = None
    for g in range(N_KEYS // PEER_TOPK):
        grp = _sorted_desc([s_ref[(g * PEER_TOPK + r) * PEER_HEADS:(g * PEER_TOPK + r + 1) * PEER_HEADS, :]
                            for r in range(PEER_TOPK)])
        best = grp if best is None else _merge_top(best, grp)
    return best


def _router_body(h_ref, w_ref, kexp_ref, s1_ref, s2_ref, b_ref, tau_ref, mlz_ref, se_ref):
    tr = h_ref.shape[0]
    q = jnp.dot(h_ref[...], w_ref[...], preferred_element_type=F32).astype(BF16)
    for c in range(2):
        s = _nt_dot(kexp_ref[c], q[:, c * KH:(c + 1) * KH]) * LOG2E
        if c == 0:
            s1_ref[...] = s
        for lc in range(tr // LANES):
            se_ref[c, lc] = s[:, lc * LANES:(lc + 1) * LANES]

    k = PEER_TOPK
    for lc in range(tr // LANES):
        lanes = slice(lc * LANES, (lc + 1) * LANES)
        for h in range(PEER_HEADS):
            s2_ref[h, :, lanes] = se_ref[1, lc, pl.ds(h, N_KEYS, stride=PEER_HEADS), :]
        a = _top16_of_keys(se_ref.at[0, lc])
        b = _top16_of_keys(se_ref.at[1, lc])
        best = [a[0] + b[r] for r in range(k)]
        lists = [[a[r1] + b[r2] for r2 in range(k // (r1 + 1))] for r1 in range(1, k // 2)]
        lists.append([a[r1] + b[0] for r1 in range(k // 2, k)])
        for lst in lists:
            best = _merge_top(best, lst)
        z = jnp.ones_like(best[0])
        for v in best[1:]:
            z = z + jnp.exp2(v - best[0])
        for r in range(k):
            b_ref[r, :, lanes] = b[r]
        tau_ref[:, lanes] = best[k - 1]
        mlz_ref[:, lanes] = best[0] + jnp.log2(z) + 1.0


def _router(h2, w_bf, kexp_bf, tr=512):
    T, D = h2.shape
    qw = w_bf.shape[1]
    return pl.pallas_call(
        _router_body,
        grid=(T // tr,),
        in_specs=[
            pl.BlockSpec((tr, D), lambda i: (i, 0)),
            pl.BlockSpec((D, qw), lambda i: (0, 0)),
            pl.BlockSpec((2, KH, KH), lambda i: (0, 0, 0)),
        ],
        out_specs=[
            pl.BlockSpec((KH, tr), lambda i: (0, i)),
            pl.BlockSpec((PEER_HEADS, N_KEYS, tr), lambda i: (0, 0, i)),
            pl.BlockSpec((PEER_TOPK, PEER_HEADS, tr), lambda i: (0, 0, i)),
            pl.BlockSpec((PEER_HEADS, tr), lambda i: (0, i)),
            pl.BlockSpec((PEER_HEADS, tr), lambda i: (0, i)),
        ],
        out_shape=[
            jax.ShapeDtypeStruct((KH, T), F32),
            jax.ShapeDtypeStruct((PEER_HEADS, N_KEYS, T), F32),
            jax.ShapeDtypeStruct((PEER_TOPK, PEER_HEADS, T), F32),
            jax.ShapeDtypeStruct((PEER_HEADS, T), F32),
            jax.ShapeDtypeStruct((PEER_HEADS, T), F32),
        ],
        scratch_shapes=[pltpu.VMEM((2, tr // LANES, KH, LANES), F32)],
        compiler_params=_params("parallel"),
        name="peer_router",
    )(h2, w_bf, kexp_bf)


def _router_weights(w_pq, keys):
    D = w_pq.shape[0]
    wp = w_pq.reshape(D, PEER_HEADS, 2, N_KEYS).transpose(0, 2, 1, 3).reshape(D, 2 * KH)
    kt = keys.transpose(1, 2, 0, 3)
    same_head = jnp.eye(PEER_HEADS, dtype=bool)[None, None, :, :, None]
    kexp = jnp.where(same_head, kt[:, :, :, None, :], 0.0).reshape(2, KH, KH)
    return wp.astype(BF16), kexp.astype(BF16)


PEER_TC = 128
PEER_TR = 64
PEER_SUB = 256


def _peer_body(ht_ref, u_ref, vt_ref, s1_ref, s2_ref, b_ref, tau_ref, mlz_ref, x_ref, o_ref,
               acc_ref, thr_ref, off_ref, g_ref, *a_refs):
    j = pl.program_id(1)
    nsub = len(a_refs)
    tt = ht_ref.shape[1]
    n_i1 = s1_ref.shape[0]
    keys_per_sub = PEER_SUB // N_KEYS
    sub8 = PEER_TR // SUBLANES

    @pl.when(j == 0)
    def _():
        acc_ref[...] = jnp.zeros_like(acc_ref)

    for il in range(n_i1):
        for tc in range(tt // PEER_TC):
            lanes = slice(tc * PEER_TC, (tc + 1) * PEER_TC)
            s1 = s1_ref[il, :, lanes]
            tau = tau_ref[:, lanes]
            thr = jnp.full_like(s1, jnp.inf)
            for r in range(PEER_TOPK):
                b = b_ref[r, :, lanes]
                thr = jnp.where(s1 + b >= tau, b, thr)
            off = s1 - mlz_ref[:, lanes]
            for h in range(PEER_HEADS):
                row = il * PEER_HEADS + h
                thr_ref[row, :, lanes] = jnp.broadcast_to(thr[h:h + 1, :], (SUBLANES, PEER_TC))
                off_ref[row, :, lanes] = jnp.broadcast_to(off[h:h + 1, :], (SUBLANES, PEER_TC))

    def scores(k):
        a_refs[k][...] = jnp.dot(u_ref[k * PEER_SUB:(k + 1) * PEER_SUB, :], ht_ref[...],
                                 preferred_element_type=F32)

    def gate(k):
        a_ref = a_refs[k]
        for il in range(keys_per_sub):
            i1 = k * keys_per_sub + il
            for tc in range(tt // PEER_TC):
                lanes = slice(tc * PEER_TC, (tc + 1) * PEER_TC)
                for r0 in range(0, N_KEYS, PEER_TR):
                    w = None
                    for h in range(PEER_HEADS):
                        row = i1 * PEER_HEADS + h
                        s2 = s2_ref[h, r0:r0 + PEER_TR, lanes].reshape(sub8, SUBLANES, PEER_TC)
                        wh = jnp.where(s2 >= thr_ref[row, :, lanes][None],
                                       jnp.exp2(s2 + off_ref[row, :, lanes][None]), 0.0)
                        w = wh if w is None else w + wh
                    a = a_ref[il * N_KEYS + r0:il * N_KEYS + r0 + PEER_TR, lanes].reshape(
                        sub8, SUBLANES, PEER_TC)
                    g = w * (a * (1.0 + lax.erf(a * SQRT_HALF)))
                    g_ref[i1 * N_KEYS + r0:i1 * N_KEYS + r0 + PEER_TR, lanes] = (
                        g.reshape(PEER_TR, PEER_TC).astype(BF16))

    for k in range(nsub):
        scores(k)
        gate(k)
    acc_ref[...] += jnp.dot(vt_ref[...], g_ref[...], preferred_element_type=F32)

    @pl.when(j == pl.num_programs(1) - 1)
    def _():
        o_ref[...] = x_ref[...] + acc_ref[...].T


def _peer_dense(xn, ht, u_all, vt_all, layer, s1, s2, b2, tau, mlz, tt=512, te=1024):
    D, T = ht.shape
    E = u_all.shape[1]
    n_i1 = te // N_KEYS
    nsub = te // PEER_SUB
    return pl.pallas_call(
        _peer_body,
        grid=(T // tt, E // te),
        in_specs=[
            pl.BlockSpec((D, tt), lambda i, j: (0, i)),
            pl.BlockSpec((None, te, D), lambda i, j: (layer, j, 0)),
            pl.BlockSpec((None, D, te), lambda i, j: (layer, 0, j)),
            pl.BlockSpec((n_i1, PEER_HEADS, tt), lambda i, j: (j, 0, i)),
            pl.BlockSpec((PEER_HEADS, N_KEYS, tt), lambda i, j: (0, 0, i)),
            pl.BlockSpec((PEER_TOPK, PEER_HEADS, tt), lambda i, j: (0, 0, i)),
            pl.BlockSpec((PEER_HEADS, tt), lambda i, j: (0, i)),
            pl.BlockSpec((PEER_HEADS, tt), lambda i, j: (0, i)),
            pl.BlockSpec((tt, D), lambda i, j: (i, 0)),
        ],
        out_specs=pl.BlockSpec((tt, D), lambda i, j: (i, 0)),
        out_shape=jax.ShapeDtypeStruct((T, D), F32),
        scratch_shapes=([pltpu.VMEM((D, tt), F32)] + [pltpu.VMEM((n_i1 * PEER_HEADS, SUBLANES, tt), F32)] * 2
                        + [pltpu.VMEM((te, tt), BF16)] + [pltpu.VMEM((PEER_SUB, tt), F32)] * nsub),
        compiler_params=_params("parallel", "arbitrary"),
        name="peer_dense",
    )(ht, u_all, vt_all, s1.reshape(N_KEYS, PEER_HEADS, T), s2, b2, tau, mlz, xn)


def _vt_body(v_ref, o_ref):
    o_ref[...] = v_ref[...].T.astype(BF16)


def _value_table_t(peer_v, te=512):
    depth, E, D = peer_v.shape
    return pl.pallas_call(
        _vt_body,
        grid=(depth, E // te),
        in_specs=[pl.BlockSpec((None, te, D), lambda l, j: (l, j, 0))],
        out_specs=pl.BlockSpec((None, D, te), lambda l, j: (l, 0, j)),
        out_shape=jax.ShapeDtypeStruct((depth, D, E), BF16),
        compiler_params=_params("parallel", "parallel"),
        name="value_table_t",
    )(peer_v)


def _rope_tables(seq):
    inv = ROPE_THETA ** (-jnp.arange(0, ROT_DIM, 2, dtype=jnp.float32) / ROT_DIM)
    ang = jnp.arange(seq, dtype=jnp.float32)[:, None] * inv[None, :]
    cos, sin = jnp.cos(ang), jnp.sin(ang)
    rest = HEAD_DIM - ROT_DIM
    zeros_h = jnp.zeros((seq, ROT_HALF), F32)
    cf = jnp.concatenate([cos, cos, jnp.ones((seq, rest), F32)], axis=-1)
    sa = jnp.concatenate([-sin, zeros_h, jnp.zeros((seq, rest), F32)], axis=-1)
    sb = jnp.concatenate([zeros_h, sin, jnp.zeros((seq, rest), F32)], axis=-1)
    return cf, sa, sb


def kernel(x, mem, attn_norm, w_in, qk_gain, lambda_qk, subln_gain, na_bias, mem_norm, w_mem_kv,
           w_out, ffn_norm, w_pq, peer_keys, peer_u, peer_v):
    B, S, D = x.shape
    depth = w_in.shape[0]
    T = B * S
    cf, sa, sb = _rope_tables(S)
    x2 = x.reshape(T, D)
    w_in_all = w_in.astype(BF16)
    u_all = peer_u.astype(BF16)
    vt_all = _value_table_t(peer_v)
    for l in range(depth):
        lam_init = 0.8 - 0.6 * math.exp(-0.3 * l)
        proj = _in_proj(x2, attn_norm[l][None], w_in_all, l, qk_gain[l], cf, sa, sb, S)
        proj3 = proj.reshape(B, S, IN_WIDTH)
        o_diff = _diff_attention(proj3, lambda_qk[l], subln_gain[l][None], lam_init)
        o_na = _na_attention(proj3, _na_bias_table(na_bias[l], S // GRID_W))
        mk, mv = _mem_kv(mem, mem_norm[l][None], w_mem_kv[l].astype(BF16), qk_gain[l][5:6])
        o_mem = _mem_attention(proj3, mk, mv)
        xn, h2, h2t = _out_proj(x2, o_diff.reshape(T, DIFF_WIDTH), o_na.reshape(T, NA_WIDTH),
                                o_mem.reshape(T, MEM_WIDTH), w_out[l].astype(BF16), ffn_norm[l][None])
        s1, s2, b2, tau, mlz = _router(h2, *_router_weights(w_pq[l], peer_keys[l]))
        x2 = _peer_dense(xn, h2t, u_all, vt_all, l, s1, s2, b2, tau, mlz)
    return x2.reshape(B, S, D)
```

```python
import functools
import math

import jax
import jax.numpy as jnp
from jax import lax
from jax.experimental import pallas as pl
from jax.experimental.pallas import tpu as pltpu

F32 = jnp.float32
BF16 = jnp.bfloat16

HEAD_DIM = 128
N_DIFF_HEADS = 4
N_NA_HEADS = 4
N_MEM_HEADS = 4
DIFF_V_DIM = 2 * HEAD_DIM
DIFF_WIDTH = N_DIFF_HEADS * DIFF_V_DIM
NA_WIDTH = N_NA_HEADS * HEAD_DIM
MEM_WIDTH = N_MEM_HEADS * HEAD_DIM
IN_WIDTH = 3 * DIFF_WIDTH + 3 * NA_WIDTH + MEM_WIDTH
ROPE_THETA = 500000.0
ROT_DIM = HEAD_DIM // 4
ROT_HALF = ROT_DIM // 2
GRID_W = 64
MAX_WIN_H = 8
WIN_W = 16
PEER_HEADS = 8
N_KEYS = 128
PEER_TOPK = 16
EPS = 1e-6
ATTN_SCALE = HEAD_DIM ** -0.5
NEG_BIG = -1e30
SQRT_HALF = math.sqrt(0.5)
LOG2E = math.log2(math.e)

LANES = 128
VMEM_LIMIT = 56 * 1024 * 1024

COL_DQ, COL_DK, COL_DV = 0, 8, 16
COL_NQ, COL_NK, COL_NV, COL_MQ = 24, 28, 32, 36


def _params(*sem):
    return pltpu.CompilerParams(dimension_semantics=sem, vmem_limit_bytes=VMEM_LIMIT)


def _rms(p, g):
    ms = jnp.mean(p * p, axis=-1, keepdims=True)
    return p * lax.rsqrt(ms + EPS) * g


def _nt_dot(a, b):
    return lax.dot_general(a, b, (((1,), (1,)), ((), ())), preferred_element_type=F32)


IN_TN = 1024


def _inproj_body(x_ref, g_ref, w_ref, gq_ref, cf_ref, sa_ref, sb_ref, o_ref, h_ref):
    j = pl.program_id(1)

    @pl.when(j == 0)
    def _():
        h_ref[...] = _rms(x_ref[...], g_ref[...]).astype(BF16)

    acc = jnp.dot(h_ref[...], w_ref[...], preferred_element_type=F32)
    ngrp = IN_TN // HEAD_DIM

    def grp(g):
        return acc[:, g * HEAD_DIM:(g + 1) * HEAD_DIM]

    def put(g, val):
        o_ref[:, g * HEAD_DIM:(g + 1) * HEAD_DIM] = val.astype(BF16)

    def rope(p):
        return (p * cf_ref[...] + pltpu.roll(p, HEAD_DIM - ROT_HALF, 1) * sa_ref[...]
                + pltpu.roll(p, ROT_HALF, 1) * sb_ref[...])

    for jj in (0, 1):
        @pl.when(j == jj)
        def _(jj=jj):
            gain = gq_ref[jj:jj + 1, :]
            for g in range(ngrp):
                put(g, rope(_rms(grp(g), gain)))

    @pl.when(j == 2)
    def _():
        o_ref[...] = acc.astype(BF16)

    @pl.when(j == 3)
    def _():
        for g in range(ngrp):
            row = 2 if g < ngrp // 2 else 3
            put(g, _rms(grp(g), gq_ref[row:row + 1, :]))

    @pl.when(j == 4)
    def _():
        for g in range(ngrp):
            if g < ngrp // 2:
                put(g, grp(g))
            else:
                put(g, _rms(grp(g), gq_ref[4:5, :]))


def _in_proj(x2, g, w_all, layer, gq, cf, sa, sb, seq, tm=1024):
    T, D = x2.shape
    nseq = seq // tm
    return pl.pallas_call(
        _inproj_body,
        grid=(T // tm, IN_WIDTH // IN_TN),
        in_specs=[
            pl.BlockSpec((tm, D), lambda i, j: (i, 0)),
            pl.BlockSpec((1, D), lambda i, j: (0, 0)),
            pl.BlockSpec((None, D, IN_TN), lambda i, j: (layer, 0, j)),
            pl.BlockSpec((6, HEAD_DIM), lambda i, j: (0, 0)),
            pl.BlockSpec((tm, HEAD_DIM), lambda i, j: (i % nseq, 0)),
            pl.BlockSpec((tm, HEAD_DIM), lambda i, j: (i % nseq, 0)),
            pl.BlockSpec((tm, HEAD_DIM), lambda i, j: (i % nseq, 0)),
        ],
        out_specs=pl.BlockSpec((tm, IN_TN), lambda i, j: (i, j)),
        out_shape=jax.ShapeDtypeStruct((T, IN_WIDTH), BF16),
        scratch_shapes=[pltpu.VMEM((tm, D), BF16)],
        compiler_params=_params("parallel", "arbitrary"),
        name="in_proj",
    )(x2, g, w_all, gq, cf, sa, sb)


DIFF_SUBQ = 256


def _diff_body(lq_ref, g_ref, q_ref, k_ref, v_ref, o_ref, *, lam_init):
    lq = lq_ref[...]
    lam = (jnp.exp(jnp.sum(lq[0:1] * lq[1:2], axis=-1, keepdims=True))
           - jnp.exp(jnp.sum(lq[2:3] * lq[3:4], axis=-1, keepdims=True)) + lam_init)

    def attend(rows, c):
        q = q_ref[rows, c * HEAD_DIM:(c + 1) * HEAD_DIM]
        k = k_ref[:, c * HEAD_DIM:(c + 1) * HEAD_DIM]
        s = _nt_dot(q, k)
        e = jnp.exp2((s - jnp.max(s, axis=-1, keepdims=True)) * (ATTN_SCALE * LOG2E))
        l = jnp.sum(e, axis=-1, keepdims=True)
        return jnp.dot(e.astype(BF16), v_ref[...], preferred_element_type=F32), l

    for r0 in range(0, q_ref.shape[0], DIFF_SUBQ):
        rows = slice(r0, r0 + DIFF_SUBQ)
        o1, l1 = attend(rows, 0)
        o2, l2 = attend(rows, 1)
        o = o1 * (1.0 / l1) - o2 * (lam / l2)
        o_ref[rows, :] = (_rms(o, g_ref[...]) * (1.0 - lam_init)).astype(o_ref.dtype)


def _diff_attention(proj3, lq, sub_g, lam_init, tq=1024):
    B, S, _ = proj3.shape
    return pl.pallas_call(
        functools.partial(_diff_body, lam_init=lam_init),
        grid=(B, N_DIFF_HEADS, S // tq),
        in_specs=[
            pl.BlockSpec((4, HEAD_DIM), lambda b, h, i: (0, 0)),
            pl.BlockSpec((1, DIFF_V_DIM), lambda b, h, i: (0, 0)),
            pl.BlockSpec((None, tq, DIFF_V_DIM), lambda b, h, i: (b, i, COL_DQ // 2 + h)),
            pl.BlockSpec((None, S, DIFF_V_DIM), lambda b, h, i: (b, 0, COL_DK // 2 + h)),
            pl.BlockSpec((None, S, DIFF_V_DIM), lambda b, h, i: (b, 0, COL_DV // 2 + h)),
        ],
        out_specs=pl.BlockSpec((None, tq, DIFF_V_DIM), lambda b, h, i: (b, i, h)),
        out_shape=jax.ShapeDtypeStruct((B, S, DIFF_WIDTH), BF16),
        compiler_params=_params("parallel", "parallel", "arbitrary"),
        name="diff_attn",
    )(lq, sub_g, proj3, proj3, proj3)


NA_QROWS = 8
NA_KROWS = 16
NA_TQ = NA_QROWS * GRID_W
NA_TK = NA_KROWS * GRID_W


def _na_table_body(b_ref, o_ref, *, rows):
    h = pl.program_id(0)
    variant = pl.program_id(1)
    qc = lax.broadcasted_iota(jnp.int32, (GRID_W, GRID_W), 0)
    kc = lax.broadcasted_iota(jnp.int32, (GRID_W, GRID_W), 1)
    cs = jnp.clip(qc - WIN_W // 2, 0, GRID_W - WIN_W)
    col_ok = (kc >= cs) & (kc < cs + WIN_W)
    dcol = kc - qc + (WIN_W - 1)
    masked = jnp.full((GRID_W, GRID_W), NEG_BIG, F32)
    col_tab = []
    for dr in range(2 * MAX_WIN_H - 1):
        t = masked
        for dc in range(2 * WIN_W - 1):
            t = jnp.where(dcol == dc, b_ref[h, dr, dc] * LOG2E, t)
        col_tab.append(jnp.where(col_ok, t, NEG_BIG))

    blocks = ((0, 0), (NA_QROWS, NA_QROWS - MAX_WIN_H // 2), (rows - NA_QROWS, rows - NA_KROWS))
    for vi, (r0, kb) in enumerate(blocks):
        @pl.when(variant == vi)
        def _(r0=r0, kb=kb):
            for qr in range(NA_QROWS):
                r = r0 + qr
                rs = min(max(r - MAX_WIN_H // 2, 0), rows - MAX_WIN_H)
                for kp in range(NA_KROWS // 2):
                    halves = []
                    for kr in (2 * kp, 2 * kp + 1):
                        ka = kb + kr
                        inside = rs <= ka < rs + MAX_WIN_H
                        halves.append(col_tab[ka - r + MAX_WIN_H - 1] if inside else masked)
                    o_ref[qr * GRID_W:(qr + 1) * GRID_W, kp * LANES:(kp + 1) * LANES] = (
                        jnp.concatenate(halves, axis=1))


def _na_bias_table(bias, rows):
    nh = bias.shape[0]
    return pl.pallas_call(
        functools.partial(_na_table_body, rows=rows),
        grid=(nh, 3),
        in_specs=[pl.BlockSpec(memory_space=pltpu.SMEM)],
        out_specs=pl.BlockSpec((None, None, NA_TQ, NA_TK), lambda h, v: (h, v, 0, 0)),
        out_shape=jax.ShapeDtypeStruct((nh, 3, NA_TQ, NA_TK), F32),
        compiler_params=_params("parallel", "arbitrary"),
        name="na_bias_table",
    )(bias)


def _na_body(q_ref, k_ref, v_ref, b_ref, o_ref, *, rows):
    i = pl.program_id(2)
    kb = jnp.clip(i * NA_QROWS - MAX_WIN_H // 2, 0, rows - NA_KROWS)
    start = pl.multiple_of(kb * GRID_W, GRID_W)
    k = k_ref[pl.ds(start, NA_TK), :]
    v = v_ref[pl.ds(start, NA_TK), :]
    t = _nt_dot(q_ref[...], k) * (ATTN_SCALE * LOG2E) + b_ref[...]
    e = jnp.exp2(t - jnp.max(t, axis=-1, keepdims=True))
    p = e * (1.0 / jnp.sum(e, axis=-1, keepdims=True))
    o_ref[...] = jnp.dot(p.astype(BF16), v, preferred_element_type=F32).astype(o_ref.dtype)


def _na_attention(proj3, bias_tab):
    B, S, _ = proj3.shape
    rows = S // GRID_W
    nblk = rows // NA_QROWS

    def bias_map(b, h, i):
        return (h, jnp.where(i == 0, 0, jnp.where(i == nblk - 1, 2, 1)), 0, 0)

    return pl.pallas_call(
        functools.partial(_na_body, rows=rows),
        grid=(B, N_NA_HEADS, nblk),
        in_specs=[
            pl.BlockSpec((None, NA_TQ, HEAD_DIM), lambda b, h, i: (b, i, COL_NQ + h)),
            pl.BlockSpec((None, S, HEAD_DIM), lambda b, h, i: (b, 0, COL_NK + h)),
            pl.BlockSpec((None, S, HEAD_DIM), lambda b, h, i: (b, 0, COL_NV + h)),
            pl.BlockSpec((None, None, NA_TQ, NA_TK), bias_map),
        ],
        out_specs=pl.BlockSpec((None, NA_TQ, HEAD_DIM), lambda b, h, i: (b, i, h)),
        out_shape=jax.ShapeDtypeStruct((B, S, NA_WIDTH), BF16),
        compiler_params=_params("parallel", "parallel", "arbitrary"),
        name="na_attn",
    )(proj3, proj3, proj3, bias_tab)


def _memkv_body(m_ref, g_ref, w_ref, gk_ref, k_ref, v_ref):
    hm = _rms(m_ref[...], g_ref[...]).astype(BF16)
    kv = jnp.dot(hm, w_ref[...], preferred_element_type=F32)
    for h in range(N_MEM_HEADS):
        sl = slice(h * HEAD_DIM, (h + 1) * HEAD_DIM)
        k_ref[:, sl] = _rms(kv[:, sl], gk_ref[...]).astype(BF16)
    v_ref[...] = kv[:, MEM_WIDTH:].astype(BF16)


def _mem_kv(mem, g, w_bf, gk):
    B, M, D = mem.shape
    return pl.pallas_call(
        _memkv_body,
        grid=(B,),
        in_specs=[
            pl.BlockSpec((None, M, D), lambda b: (b, 0, 0)),
            pl.BlockSpec((1, D), lambda b: (0, 0)),
            pl.BlockSpec((D, 2 * MEM_WIDTH), lambda b: (0, 0)),
            pl.BlockSpec((1, HEAD_DIM), lambda b: (0, 0)),
        ],
        out_specs=[pl.BlockSpec((None, M, MEM_WIDTH), lambda b: (b, 0, 0))] * 2,
        out_shape=[jax.ShapeDtypeStruct((B, M, MEM_WIDTH), BF16)] * 2,
        compiler_params=_params("parallel"),
        name="mem_kv",
    )(mem, g, w_bf, gk)


def _memattn_body(q_ref, k_ref, v_ref, o_ref):
    for h in range(N_MEM_HEADS):
        sl = slice(h * HEAD_DIM, (h + 1) * HEAD_DIM)
        s = _nt_dot(q_ref[:, sl], k_ref[:, sl])
        e = jnp.exp2((s - jnp.max(s, axis=-1, keepdims=True)) * (ATTN_SCALE * LOG2E))
        p = e * (1.0 / jnp.sum(e, axis=-1, keepdims=True))
        o_ref[:, sl] = jnp.dot(p.astype(BF16), v_ref[:, sl], preferred_element_type=F32).astype(o_ref.dtype)


def _mem_attention(proj3, mk, mv, tq=512):
    B, S, _ = proj3.shape
    M = mk.shape[1]
    return pl.pallas_call(
        _memattn_body,
        grid=(B, S // tq),
        in_specs=[
            pl.BlockSpec((None, tq, MEM_WIDTH), lambda b, i: (b, i, COL_MQ // N_MEM_HEADS)),
            pl.BlockSpec((None, M, MEM_WIDTH), lambda b, i: (b, 0, 0)),
            pl.BlockSpec((None, M, MEM_WIDTH), lambda b, i: (b, 0, 0)),
        ],
        out_specs=pl.BlockSpec((None, tq, MEM_WIDTH), lambda b, i: (b, i, 0)),
        out_shape=jax.ShapeDtypeStruct((B, S, MEM_WIDTH), BF16),
        compiler_params=_params("parallel", "arbitrary"),
        name="mem_attn",
    )(proj3, mk, mv)


def _outproj_body(x_ref, od_ref, on_ref, om_ref, w_ref, g_ref, xn_ref, h_ref, ht_ref):
    mix = jnp.concatenate([od_ref[...], on_ref[...], om_ref[...]], axis=-1)
    xn = x_ref[...] + jnp.dot(mix, w_ref[...], preferred_element_type=F32)
    xn_ref[...] = xn
    h = _rms(xn, g_ref[...])
    h_ref[...] = h.astype(BF16)
    ht_ref[...] = h.T.astype(BF16)


def _out_proj(x2, od, on, om, w_bf, g, tm=512):
    T, D = x2.shape
    return pl.pallas_call(
        _outproj_body,
        grid=(T // tm,),
        in_specs=[
            pl.BlockSpec((tm, D), lambda i: (i, 0)),
            pl.BlockSpec((tm, DIFF_WIDTH), lambda i: (i, 0)),
            pl.BlockSpec((tm, NA_WIDTH), lambda i: (i, 0)),
            pl.BlockSpec((tm, MEM_WIDTH), lambda i: (i, 0)),
            pl.BlockSpec((D, D), lambda i: (0, 0)),
            pl.BlockSpec((1, D), lambda i: (0, 0)),
        ],
        out_specs=[pl.BlockSpec((tm, D), lambda i: (i, 0))] * 2 + [pl.BlockSpec((D, tm), lambda i: (0, i))],
        out_shape=[jax.ShapeDtypeStruct((T, D), F32), jax.ShapeDtypeStruct((T, D), BF16),
                   jax.ShapeDtypeStruct((D, T), BF16)],
        compiler_params=_params("parallel"),
        name="out_proj",
    )(x2, od, on, om, w_bf, g)


KH = N_KEYS * PEER_HEADS


def _sort_pairs(n):
    pairs = []

    def merge(lo, hi, r):
        step = r * 2
        if step < hi - lo:
            merge(lo, hi, step)
            merge(lo + r, hi, step)
            pairs.extend((i, i + r) for i in range(lo + r, hi - r, step))
        else:
            pairs.append((lo, lo + r))

    def sort(lo, hi):
        if hi > lo:
            mid = lo + (hi - lo) // 2
            sort(lo, mid)
            sort(mid + 1, hi)
            merge(lo, hi, 1)

    sort(0, n - 1)
    return pairs


_SORT16 = _sort_pairs(PEER_TOPK)


def _exchange(v, i, j):
    v[i], v[j] = jnp.maximum(v[i], v[j]), jnp.minimum(v[i], v[j])


def _sorted_desc(v):
    v = list(v)
    for i, j in _SORT16:
        _exchange(v, i, j)
    return v


def _merge_top(a, b):
    k = PEER_TOPK
    c = list(a)
    for i in range(len(b)):
        c[k - 1 - i] = jnp.maximum(a[k - 1 - i], b[i])
    d = k // 2
    while d >= 1:
        for i in range(k):
            if i & d == 0:
                _exchange(c, i, i + d)
        d //= 2
    return c


def _top16_of_keys(s_ref):
    best = None
    for g in range(N_KEYS // PEER_TOPK):
        grp = _sorted_desc([s_ref[(g * PEER_TOPK + r) * PEER_HEADS:(g * PEER_TOPK + r + 1) * PEER_HEADS, :]
                            for r in range(PEER_TOPK)])
        best = grp if best is None else _merge_top(best, grp)
    return best


def _router_body(h_ref, w_ref, kexp_ref, s1_ref, s2_ref, b_ref, tau_ref, mlz_ref, se_ref):
    tr = h_ref.shape[0]
    q = jnp.dot(h_ref[...], w_ref[...], preferred_element_type=F32).astype(BF16)
    for c in range(2):
        s = _nt_dot(kexp_ref[c], q[:, c * KH:(c + 1) * KH]) * LOG2E
        if c == 0:
            s1_ref[...] = s
        for lc in range(tr // LANES):
            se_ref[c, lc] = s[:, lc * LANES:(lc + 1) * LANES]

    k = PEER_TOPK
    for lc in range(tr // LANES):
        lanes = slice(lc * LANES, (lc + 1) * LANES)
        for h in range(PEER_HEADS):
            s2_ref[h, :, lanes] = se_ref[1, lc, pl.ds(h, N_KEYS, stride=PEER_HEADS), :]
        a = _top16_of_keys(se_ref.at[0, lc])
        b = _top16_of_keys(se_ref.at[1, lc])
        best = [a[0] + b[r] for r in range(k)]
        lists = [[a[r1] + b[r2] for r2 in range(k // (r1 + 1))] for r1 in range(1, k // 2)]
        lists.append([a[r1] + b[0] for r1 in range(k // 2, k)])
        for lst in lists:
            best = _merge_top(best, lst)
        z = jnp.ones_like(best[0])
        for v in best[1:]:
            z = z + jnp.exp2(v - best[0])
        for r in range(k):
            b_ref[r, :, lanes] = b[r]
        tau_ref[:, lanes] = best[k - 1]
        mlz_ref[:, lanes] = best[0] + jnp.log2(z) + 1.0


def _router(h2, w_bf, kexp_bf, tr=256):
    T, D = h2.shape
    qw = w_bf.shape[1]
    return pl.pallas_call(
        _router_body,
        grid=(T // tr,),
        in_specs=[
            pl.BlockSpec((tr, D), lambda i: (i, 0)),
            pl.BlockSpec((D, qw), lambda i: (0, 0)),
            pl.BlockSpec((2, KH, KH), lambda i: (0, 0, 0)),
        ],
        out_specs=[
            pl.BlockSpec((KH, tr), lambda i: (0, i)),
            pl.BlockSpec((PEER_HEADS, N_KEYS, tr), lambda i: (0, 0, i)),
            pl.BlockSpec((PEER_TOPK, PEER_HEADS, tr), lambda i: (0, 0, i)),
            pl.BlockSpec((PEER_HEADS, tr), lambda i: (0, i)),
            pl.BlockSpec((PEER_HEADS, tr), lambda i: (0, i)),
        ],
        out_shape=[
            jax.ShapeDtypeStruct((KH, T), F32),
            jax.ShapeDtypeStruct((PEER_HEADS, N_KEYS, T), F32),
            jax.ShapeDtypeStruct((PEER_TOPK, PEER_HEADS, T), F32),
            jax.ShapeDtypeStruct((PEER_HEADS, T), F32),
            jax.ShapeDtypeStruct((PEER_HEADS, T), F32),
        ],
        scratch_shapes=[pltpu.VMEM((2, tr // LANES, KH, LANES), F32)],
        compiler_params=_params("parallel"),
        name="peer_router",
    )(h2, w_bf, kexp_bf)


def _router_weights(w_pq, keys):
    D = w_pq.shape[0]
    wp = w_pq.reshape(D, PEER_HEADS, 2, N_KEYS).transpose(0, 2, 1, 3).reshape(D, 2 * KH)
    kt = keys.transpose(1, 2, 0, 3)
    same_head = jnp.eye(PEER_HEADS, dtype=bool)[None, None, :, :, None]
    kexp = jnp.where(same_head, kt[:, :, :, None, :], 0.0).reshape(2, KH, KH)
    return wp.astype(BF16), kexp.astype(BF16)


PEER_TC = 128
PEER_TR = 64
PEER_SUB = 256


def _peer_body(ht_ref, u_ref, vt_ref, s1_ref, s2_ref, b_ref, tau_ref, mlz_ref, x_ref, o_ref,
               acc_ref, thr_ref, off_ref, g_ref, *a_refs):
    j = pl.program_id(1)
    nsub = len(a_refs)
    tt = ht_ref.shape[1]
    n_i1 = s1_ref.shape[0]
    keys_per_sub = PEER_SUB // N_KEYS
    sub8 = PEER_TR // 8

    @pl.when(j == 0)
    def _():
        acc_ref[...] = jnp.zeros_like(acc_ref)

    for il in range(n_i1):
        for tc in range(tt // PEER_TC):
            lanes = slice(tc * PEER_TC, (tc + 1) * PEER_TC)
            s1 = s1_ref[il, :, lanes]
            tau = tau_ref[:, lanes]
            thr = jnp.full_like(s1, jnp.inf)
            for r in range(PEER_TOPK):
                b = b_ref[r, :, lanes]
                thr = jnp.where(s1 + b >= tau, b, thr)
            off = s1 - mlz_ref[:, lanes]
            for h in range(PEER_HEADS):
                row = il * PEER_HEADS + h
                thr_ref[row, :, lanes] = jnp.broadcast_to(thr[h:h + 1, :], (8, PEER_TC))
                off_ref[row, :, lanes] = jnp.broadcast_to(off[h:h + 1, :], (8, PEER_TC))

    def scores(k):
        a_refs[k][...] = jnp.dot(u_ref[k * PEER_SUB:(k + 1) * PEER_SUB, :], ht_ref[...],
                                 preferred_element_type=F32)

    def gate(k):
        a_ref = a_refs[k]
        for il in range(keys_per_sub):
            i1 = k * keys_per_sub + il
            for tc in range(tt // PEER_TC):
                lanes = slice(tc * PEER_TC, (tc + 1) * PEER_TC)
                for r0 in range(0, N_KEYS, PEER_TR):
                    w = None
                    for h in range(PEER_HEADS):
                        row = i1 * PEER_HEADS + h
                        s2 = s2_ref[h, r0:r0 + PEER_TR, lanes].reshape(sub8, 8, PEER_TC)
                        wh = jnp.where(s2 >= thr_ref[row, :, lanes][None],
                                       jnp.exp2(s2 + off_ref[row, :, lanes][None]), 0.0)
                        w = wh if w is None else w + wh
                    a = a_ref[il * N_KEYS + r0:il * N_KEYS + r0 + PEER_TR, lanes].reshape(sub8, 8, PEER_TC)
                    g = w * (a * (1.0 + lax.erf(a * SQRT_HALF)))
                    g_ref[i1 * N_KEYS + r0:i1 * N_KEYS + r0 + PEER_TR, lanes] = (
                        g.reshape(PEER_TR, PEER_TC).astype(BF16))

    for k in range(nsub):
        scores(k)
        gate(k)
    acc_ref[...] += lax.dot_general(vt_ref[...], g_ref[...], (((0,), (0,)), ((), ())),
                                    preferred_element_type=F32)

    @pl.when(j == pl.num_programs(1) - 1)
    def _():
        o_ref[...] = x_ref[...] + acc_ref[...].T


def _peer_dense(xn, ht, u_all, vt_all, layer, s1, s2, b2, tau, mlz, tt=512, te=1024):
    D, T = ht.shape
    E = u_all.shape[1]
    n_i1 = te // N_KEYS
    nsub = te // PEER_SUB
    return pl.pallas_call(
        _peer_body,
        grid=(T // tt, E // te),
        in_specs=[
            pl.BlockSpec((D, tt), lambda i, j: (0, i)),
            pl.BlockSpec((None, te, D), lambda i, j: (layer, j, 0)),
            pl.BlockSpec((None, te, D), lambda i, j: (layer, j, 0)),
            pl.BlockSpec((n_i1, PEER_HEADS, tt), lambda i, j: (j, 0, i)),
            pl.BlockSpec((PEER_HEADS, N_KEYS, tt), lambda i, j: (0, 0, i)),
            pl.BlockSpec((PEER_TOPK, PEER_HEADS, tt), lambda i, j: (0, 0, i)),
            pl.BlockSpec((PEER_HEADS, tt), lambda i, j: (0, i)),
            pl.BlockSpec((PEER_HEADS, tt), lambda i, j: (0, i)),
            pl.BlockSpec((tt, D), lambda i, j: (i, 0)),
        ],
        out_specs=pl.BlockSpec((tt, D), lambda i, j: (i, 0)),
        out_shape=jax.ShapeDtypeStruct((T, D), F32),
        scratch_shapes=([pltpu.VMEM((D, tt), F32)] + [pltpu.VMEM((n_i1 * PEER_HEADS, 8, tt), F32)] * 2
                        + [pltpu.VMEM((te, tt), BF16)] + [pltpu.VMEM((PEER_SUB, tt), F32)] * nsub),
        compiler_params=_params("parallel", "arbitrary"),
        name="peer_dense",
    )(ht, u_all, vt_all, s1.reshape(N_KEYS, PEER_HEADS, T), s2, b2, tau, mlz, xn)


def _vt_body(v_ref, o_ref):
    o_ref[...] = v_ref[...].T.astype(BF16)


def _value_table_t(peer_v, te=512):
    depth, E, D = peer_v.shape
    return pl.pallas_call(
        _vt_body,
        grid=(depth, E // te),
        in_specs=[pl.BlockSpec((None, te, D), lambda l, j: (l, j, 0))],
        out_specs=pl.BlockSpec((None, D, te), lambda l, j: (l, 0, j)),
        out_shape=jax.ShapeDtypeStruct((depth, D, E), BF16),
        compiler_params=_params("parallel", "parallel"),
        name="value_table_t",
    )(peer_v)


def _rope_tables(seq):
    inv = ROPE_THETA ** (-jnp.arange(0, ROT_DIM, 2, dtype=jnp.float32) / ROT_DIM)
    ang = jnp.arange(seq, dtype=jnp.float32)[:, None] * inv[None, :]
    cos, sin = jnp.cos(ang), jnp.sin(ang)
    rest = HEAD_DIM - ROT_DIM
    zeros_h = jnp.zeros((seq, ROT_HALF), F32)
    cf = jnp.concatenate([cos, cos, jnp.ones((seq, rest), F32)], axis=-1)
    sa = jnp.concatenate([-sin, zeros_h, jnp.zeros((seq, rest), F32)], axis=-1)
    sb = jnp.concatenate([zeros_h, sin, jnp.zeros((seq, rest), F32)], axis=-1)
    return cf, sa, sb


def kernel(x, mem, attn_norm, w_in, qk_gain, lambda_qk, subln_gain, na_bias, mem_norm, w_mem_kv,
           w_out, ffn_norm, w_pq, peer_keys, peer_u, peer_v):
    B, S, D = x.shape
    depth = w_in.shape[0]
    T = B * S
    cf, sa, sb = _rope_tables(S)
    x2 = x.reshape(T, D)
    w_in_all = w_in.astype(BF16)
    u_all = peer_u.astype(BF16)
    vt_all = peer_v.astype(BF16)
    for l in range(depth):
        lam_init = 0.8 - 0.6 * math.exp(-0.3 * l)
        proj = _in_proj(x2, attn_norm[l][None], w_in_all, l, qk_gain[l], cf, sa, sb, S)
        proj3 = proj.reshape(B, S, IN_WIDTH)
        o_diff = _diff_attention(proj3, lambda_qk[l], subln_gain[l][None], lam_init)
        o_na = _na_attention(proj3, _na_bias_table(na_bias[l], S // GRID_W))
        mk, mv = _mem_kv(mem, mem_norm[l][None], w_mem_kv[l].astype(BF16), qk_gain[l][5:6])
        o_mem = _mem_attention(proj3, mk, mv)
        xn, h2, h2t = _out_proj(x2, o_diff.reshape(T, DIFF_WIDTH), o_na.reshape(T, NA_WIDTH),
                                o_mem.reshape(T, MEM_WIDTH), w_out[l].astype(BF16), ffn_norm[l][None])
        s1, s2, b2, tau, mlz = _router(h2, *_router_weights(w_pq[l], peer_keys[l]))
        x2 = _peer_dense(xn, h2t, u_all, vt_all, l, s1, s2, b2, tau, mlz)
    return x2.reshape(B, S, D)
```
